```python
import jax, jax.numpy as jnp
from jax import lax
import numpy as np

D_MODEL = 1024
BATCH = 2
SEQ = 8192
DEPTH = 2

MOBA_HEADS = 8
HEAD_DIM = D_MODEL // MOBA_HEADS
ROT_DIM = HEAD_DIM // 4
ROPE_THETA = 500000.0
MOBA_BLOCK = 256
MOBA_TOPK = 3
MOBA_Q_CHUNK = 32
DSA_HEADS = 8
DSA_KV_HEADS = 2
DSA_TOPK_MAX = 256
DSA_Q_CHUNK = 128
IDX_HEADS = 8
IDX_DIM = 64
IDX_ROT_DIM = IDX_DIM // 4
D_FF = -(-8 * D_MODEL // (3 * 256)) * 256
MOBA_W = MOBA_HEADS * HEAD_DIM
DSA_QW = DSA_HEADS * HEAD_DIM
DSA_KVW = DSA_KV_HEADS * HEAD_DIM
IDX_QW = IDX_HEADS * IDX_DIM
IN_SIZES = (MOBA_W, MOBA_W, MOBA_W, DSA_QW, DSA_KVW, DSA_KVW, IDX_QW, IDX_DIM, IDX_HEADS, D_MODEL, D_MODEL)
N_IN = 3 * MOBA_W + DSA_QW + 2 * DSA_KVW + IDX_QW + IDX_DIM + IDX_HEADS + 2 * D_MODEL
N_MOD = 6
NORM_EPS = 1e-6
NEG_INF = -1e30

kernel_name = "hybrid_moba_dsa_gated_block"


def _rmsnorm(x, g):
    xf = x.astype(jnp.float32)
    y = xf * lax.rsqrt(jnp.mean(xf * xf, axis=-1, keepdims=True) + NORM_EPS)
    return (y * g.astype(jnp.float32)).astype(x.dtype)


def _partial_rope(x, pos, rot_dim):
    half = rot_dim // 2
    inv_freq = ROPE_THETA ** (-jnp.arange(half, dtype=jnp.float32) * 2.0 / rot_dim)
    ang = pos.astype(jnp.float32)[..., None] * inv_freq
    cos = jnp.cos(ang)[:, :, None, :]
    sin = jnp.sin(ang)[:, :, None, :]
    xf = x.astype(jnp.float32)
    x1 = xf[..., :half]
    x2 = xf[..., half:rot_dim]
    out = jnp.concatenate([x1 * cos - x2 * sin, x2 * cos + x1 * sin, xf[..., rot_dim:]], axis=-1)
    return out.astype(x.dtype)


def _moba_attention(q, k, v):
    bsz, seq, nh, dh = q.shape
    nb = -(-seq // MOBA_BLOCK)
    n_sel = min(MOBA_TOPK, nb)
    pad = nb * MOBA_BLOCK - seq
    kp = jnp.pad(k, ((0, 0), (0, pad), (0, 0), (0, 0)))
    vp = jnp.pad(v, ((0, 0), (0, pad), (0, 0), (0, 0)))
    kbh = kp.reshape(bsz, nb, MOBA_BLOCK, nh, dh).transpose(0, 3, 1, 2, 4)
    vbh = vp.reshape(bsz, nb, MOBA_BLOCK, nh, dh).transpose(0, 3, 1, 2, 4)
    kmean = jnp.mean(kbh.astype(jnp.float32), axis=3)
    n_chunks = seq // MOBA_Q_CHUNK
    qc = q.reshape(bsz, n_chunks, MOBA_Q_CHUNK, nh, dh).swapaxes(0, 1)
    bi = jnp.arange(bsz)[:, None, None, None]
    hi = jnp.arange(nh)[None, None, :, None]
    blk_ids = jnp.arange(nb)
    scale = dh ** -0.5

    def one_chunk(args):
        ci, qb = args
        start = ci * MOBA_Q_CHUNK
        q_pos = start + jnp.arange(MOBA_Q_CHUNK)
        own = start // MOBA_BLOCK
        gate = jnp.einsum('bqhd,bhnd->bqhn', qb.astype(jnp.float32), kmean)
        gate = jnp.where(blk_ids < own, gate, NEG_INF)
        _, sel = lax.top_k(gate, n_sel)
        sel_ok = sel < own
        k_sel = kbh[bi, hi, sel]
        v_sel = vbh[bi, hi, sel]
        k_own = lax.dynamic_slice_in_dim(kp, own * MOBA_BLOCK, MOBA_BLOCK, axis=1)
        v_own = lax.dynamic_slice_in_dim(vp, own * MOBA_BLOCK, MOBA_BLOCK, axis=1)
        own_pos = own * MOBA_BLOCK + jnp.arange(MOBA_BLOCK)
        causal = own_pos[None, :] <= q_pos[:, None]
        s_sel = jnp.einsum('bqhd,bqhjkd->bqhjk', qb, k_sel).astype(jnp.float32) * scale
        s_sel = jnp.where(sel_ok[..., None], s_sel, NEG_INF)
        s_sel = s_sel.reshape(bsz, MOBA_Q_CHUNK, nh, n_sel * MOBA_BLOCK)
        s_own = jnp.einsum('bqhd,bkhd->bqhk', qb, k_own).astype(jnp.float32) * scale
        s_own = jnp.where(causal[None, :, None, :], s_own, NEG_INF)
        p = jax.nn.softmax(jnp.concatenate([s_sel, s_own], axis=-1), axis=-1).astype(v.dtype)
        p_sel = p[..., :n_sel * MOBA_BLOCK].reshape(bsz, MOBA_Q_CHUNK, nh, n_sel, MOBA_BLOCK)
        p_own = p[..., n_sel * MOBA_BLOCK:]
        return (jnp.einsum('bqhjk,bqhjkd->bqhd', p_sel, v_sel)
                + jnp.einsum('bqhk,bkhd->bqhd', p_own, v_own))

    out = lax.map(one_chunk, (jnp.arange(n_chunks), qc))
    return out.swapaxes(0, 1).reshape(bsz, seq, nh * dh)


def _dsa_attention(q, k, v, qi, ki, wi):
    bsz, seq, nq, dh = q.shape
    nkv = k.shape[2]
    grp = nq // nkv
    n_top = min(DSA_TOPK_MAX, seq // 4)
    n_chunks = seq // DSA_Q_CHUNK
    qc = q.reshape(bsz, n_chunks, DSA_Q_CHUNK, nq, dh).swapaxes(0, 1)
    qic = qi.reshape(bsz, n_chunks, DSA_Q_CHUNK, IDX_HEADS, IDX_DIM).swapaxes(0, 1)
    wic = wi.reshape(bsz, n_chunks, DSA_Q_CHUNK, IDX_HEADS).swapaxes(0, 1)
    kif = ki.astype(jnp.float32)
    key_pos = jnp.arange(seq)
    bi = jnp.arange(bsz)[:, None, None]
    scale = dh ** -0.5
    idx_scale = IDX_DIM ** -0.5

    def one_chunk(args):
        ci, qb, qib, wib = args
        q_pos = ci * DSA_Q_CHUNK + jnp.arange(DSA_Q_CHUNK)
        logits = jnp.einsum('bqhd,bsd->bqhs', qib.astype(jnp.float32), kif) * idx_scale
        score = jnp.einsum('bqh,bqhs->bqs', wib.astype(jnp.float32), jax.nn.relu(logits))
        score = jnp.where(key_pos[None, None, :] <= q_pos[None, :, None], score, NEG_INF)
        _, idx = lax.top_k(score, n_top)
        valid = idx <= q_pos[None, :, None]
        k_sel = k[bi, idx]
        v_sel = v[bi, idx]
        qg = qb.reshape(bsz, DSA_Q_CHUNK, nkv, grp, dh)
        s = jnp.einsum('bqgrd,bqkgd->bqgrk', qg, k_sel).astype(jnp.float32) * scale
        s = jnp.where(valid[:, :, None, None, :], s, NEG_INF)
        p = jax.nn.softmax(s, axis=-1).astype(v.dtype)
        o = jnp.einsum('bqgrk,bqkgd->bqgrd', p, v_sel)
        return o.reshape(bsz, DSA_Q_CHUNK, nq, dh)

    out = lax.map(one_chunk, (jnp.arange(n_chunks), qc, qic, wic))
    return out.swapaxes(0, 1).reshape(bsz, seq, nq * dh)


def _layer(x, c, positions, w_ada, b_ada, g_pre_mix, g_post_mix, w_in, w_out,
           g_pre_ffn, g_post_ffn, w_gate, w_up, w_down):
    bsz, seq, _ = x.shape
    mod = jax.nn.silu(c) @ w_ada + b_ada
    sh1, sc1, gt1, sh2, sc2, gt2 = [m[:, None, :] for m in jnp.split(mod, N_MOD, axis=-1)]

    h = _rmsnorm(x, g_pre_mix) * (1.0 + sc1) + sh1
    proj = h @ w_in
    cuts = [int(v) for v in np.cumsum(IN_SIZES)[:-1]]
    (qa, ka, va, qb, kb, vb, qi, ki, wi, ga, gb) = jnp.split(proj, cuts, axis=-1)
    qa = _partial_rope(qa.reshape(bsz, seq, MOBA_HEADS, HEAD_DIM), positions, ROT_DIM)
    ka = _partial_rope(ka.reshape(bsz, seq, MOBA_HEADS, HEAD_DIM), positions, ROT_DIM)
    va = va.reshape(bsz, seq, MOBA_HEADS, HEAD_DIM)
    qb = _partial_rope(qb.reshape(bsz, seq, DSA_HEADS, HEAD_DIM), positions, ROT_DIM)
    kb = _partial_rope(kb.reshape(bsz, seq, DSA_KV_HEADS, HEAD_DIM), positions, ROT_DIM)
    vb = vb.reshape(bsz, seq, DSA_KV_HEADS, HEAD_DIM)
    qi = _partial_rope(qi.reshape(bsz, seq, IDX_HEADS, IDX_DIM), positions, IDX_ROT_DIM)
    ki = _partial_rope(ki[:, :, None, :], positions, IDX_ROT_DIM)[:, :, 0, :]
    wi = wi * (IDX_HEADS ** -0.5)
    y_a = _moba_attention(qa, ka, va)
    y_b = _dsa_attention(qb, kb, vb, qi, ki, wi)
    mixed = jax.nn.sigmoid(ga) * y_a + jax.nn.sigmoid(gb) * y_b
    x = x + gt1 * _rmsnorm(mixed @ w_out, g_post_mix)

    h = _rmsnorm(x, g_pre_ffn) * (1.0 + sc2) + sh2
    f = (jax.nn.silu(h @ w_gate) * (h @ w_up)) @ w_down
    return x + gt2 * _rmsnorm(f, g_post_ffn)


def setup_inputs(seed: int = 0) -> dict:
    key = jax.random.key(seed)
    ks = jax.random.split(key, 16)
    f32 = jnp.float32
    x = jax.random.normal(ks[0], (BATCH, SEQ, D_MODEL), f32)
    c = jax.random.normal(ks[1], (BATCH, D_MODEL), f32)
    offset = jax.random.randint(ks[2], (BATCH, 1), 0, 4096, dtype=jnp.int32)
    positions = (offset + jnp.arange(SEQ, dtype=jnp.int32)[None, :]).astype(jnp.int32)
    def nrm(k, shape, fan_in):
        return jax.random.normal(k, shape, f32) * (fan_in ** -0.5)
    def gain(k):
        return 1.0 + 0.05 * jax.random.normal(k, (DEPTH, D_MODEL), f32)
    return {
        "x": x,
        "c": c,
        "positions": positions,
        "w_ada": nrm(ks[3], (DEPTH, D_MODEL, N_MOD * D_MODEL), D_MODEL),
        "b_ada": 0.01 * jax.random.normal(ks[4], (DEPTH, N_MOD * D_MODEL), f32),
        "g_pre_mix": gain(ks[5]),
        "g_post_mix": gain(ks[6]),
        "w_in": nrm(ks[7], (DEPTH, D_MODEL, N_IN), D_MODEL),
        "w_out": nrm(ks[8], (DEPTH, D_MODEL, D_MODEL), D_MODEL),
        "g_pre_ffn": gain(ks[9]),
        "g_post_ffn": gain(ks[10]),
        "w_gate": nrm(ks[11], (DEPTH, D_MODEL, D_FF), D_MODEL),
        "w_up": nrm(ks[12], (DEPTH, D_MODEL, D_FF), D_MODEL),
        "w_down": nrm(ks[13], (DEPTH, D_FF, D_MODEL), D_FF),
    }


def reference(x, c, positions, w_ada, b_ada, g_pre_mix, g_post_mix, w_in, w_out,
              g_pre_ffn, g_post_ffn, w_gate, w_up, w_down):
    for l in range(DEPTH):
        x = _layer(x, c, positions, w_ada[l], b_ada[l], g_pre_mix[l], g_post_mix[l],
                   w_in[l], w_out[l], g_pre_ffn[l], g_post_ffn[l],
                   w_gate[l], w_up[l], w_down[l])
    return x
```

```python
import functools

import jax
import jax.numpy as jnp
import numpy as np
from jax import lax
from jax.experimental import pallas as pl
from jax.experimental.pallas import tpu as pltpu

F32 = jnp.float32
BF16 = jnp.bfloat16
I32 = jnp.int32

D_MODEL = 1024
HEAD_DIM = 128
ROT_DIM = HEAD_DIM // 4
ROPE_THETA = 500000.0
MOBA_HEADS = 8
MOBA_BLOCK = 256
MOBA_TOPK = 3
DSA_HEADS = 8
DSA_KV_HEADS = 2
DSA_GROUP = DSA_HEADS // DSA_KV_HEADS
DSA_TOPK_MAX = 256
IDX_HEADS = 8
IDX_DIM = 64
IDX_ROT_DIM = IDX_DIM // 4
D_FF = 2816
N_MOD = 6
NORM_EPS = 1e-6
NEG_INF = -1e30

LANES = 128
VMEM_LIMIT = 56 * 1024 * 1024

_OFF_QA, _OFF_KA, _OFF_VA, _OFF_QB = 0, 1024, 2048, 3072
_OFF_KB, _OFF_VB, _OFF_QI, _OFF_KI, _OFF_WI, _OFF_GA, _OFF_GB = 4096, 4352, 4608, 5120, 5184, 5192, 6216

INT_MIN = -2147483648
INT_MAX = 2147483647


def _params(n_axes):
    return pltpu.CompilerParams(dimension_semantics=("arbitrary",) * n_axes,
                                vmem_limit_bytes=VMEM_LIMIT)


def _resident(block_shape, index_map):
    return pl.BlockSpec(block_shape, index_map, pipeline_mode=pl.Buffered(1))


def _ada_kernel(c_ref, w_ref, b_ref, o_ref):
    c = c_ref[...]
    s = c * jax.nn.sigmoid(c)
    o_ref[0] = jnp.dot(s, w_ref[0], preferred_element_type=F32) + b_ref[0]


def _ada(c_pad, w_ada, b_ada):
    depth, d, n = w_ada.shape
    tn = 1536
    rows = c_pad.shape[0]
    return pl.pallas_call(
        _ada_kernel,
        grid=(depth, n // tn),
        in_specs=[pl.BlockSpec((rows, d), lambda l, j: (0, 0)),
                  pl.BlockSpec((1, d, tn), lambda l, j: (l, 0, j)),
                  pl.BlockSpec((1, 1, tn), lambda l, j: (l, 0, j))],
        out_specs=pl.BlockSpec((1, rows, tn), lambda l, j: (l, 0, j)),
        out_shape=jax.ShapeDtypeStruct((depth, rows, n), F32),
        compiler_params=_params(2),
        name="ada_mod",
    )(c_pad, w_ada, b_ada.reshape(depth, 1, n))


def _rope_table_kernel(pos_ref, inv_ref, sgn_ref, cm_ref, sm_ref, ci_ref, si_ref):
    pos = pos_ref[0].astype(F32)
    ang_m = pos * inv_ref[0:1, :]
    ang_i = pos * inv_ref[1:2, :]
    cm_ref[0] = jnp.cos(ang_m)
    sm_ref[0] = jnp.sin(ang_m) * sgn_ref[0:1, :]
    ci_ref[0] = jnp.cos(ang_i)
    si_ref[0] = jnp.sin(ang_i) * sgn_ref[1:2, :]


def _rope_lane_constants():
    lane = np.arange(LANES)
    half_m, half_i = ROT_DIM // 2, IDX_ROT_DIM // 2
    inv_m = ROPE_THETA ** (-jnp.arange(half_m, dtype=F32) * 2.0 / ROT_DIM)
    inv_i = ROPE_THETA ** (-jnp.arange(half_i, dtype=F32) * 2.0 / IDX_ROT_DIM)
    in_m = lane < ROT_DIM
    in_i = (lane % IDX_DIM) < IDX_ROT_DIM
    inv = jnp.stack([jnp.where(in_m, inv_m[lane % half_m], 0.0),
                     jnp.where(in_i, inv_i[lane % half_i], 0.0)]).astype(F32)
    sgn_m = np.where(lane < half_m, -1.0, 1.0)
    sgn_i = np.where((lane % IDX_DIM) < half_i, -1.0, 1.0)
    sgn = jnp.asarray(np.stack([sgn_m, sgn_i]), F32)
    return inv, sgn


def _rope_tables(positions):
    b, s = positions.shape
    tm = min(s, 1024)
    inv, sgn = _rope_lane_constants()
    tab = jax.ShapeDtypeStruct((b, s, LANES), F32)
    tspec = pl.BlockSpec((1, tm, LANES), lambda bi, i: (bi, i, 0))
    cspec = pl.BlockSpec((2, LANES), lambda bi, i: (0, 0))
    return pl.pallas_call(
        _rope_table_kernel,
        grid=(b, s // tm),
        in_specs=[pl.BlockSpec((1, tm, 1), lambda bi, i: (bi, i, 0)), cspec, cspec],
        out_specs=[tspec] * 4,
        out_shape=[tab] * 4,
        compiler_params=_params(2),
        name="rope_tables",
    )(positions.reshape(b, s, 1), inv, sgn)


def _rope(t, cos, sin, period, half):
    lane = lax.broadcasted_iota(I32, t.shape, 1)
    first = (lane % period) < half
    partner = jnp.where(first, pltpu.roll(t, LANES - half, 1), pltpu.roll(t, half, 1))
    return t * cos + partner * sin


_IN_TILE = 256


def _inproj_kernel(x_ref, sc_ref, sh_ref, g_ref, w_ref, cm_ref, sm_ref, ci_ref, si_ref,
                   qa_ref, ka_ref, va_ref, qb_ref, kb_ref, vb_ref, qi_ref, klo_ref, khi_ref,
                   wi_ref, ga_ref, gb_ref, h_ref):
    x = x_ref[0]
    ms = jnp.mean(x * x, axis=-1, keepdims=True)
    y = x * lax.rsqrt(ms + NORM_EPS) * g_ref[...]
    h_ref[...] = (y * (1.0 + sc_ref[0]) + sh_ref[0]).astype(BF16)

    def proj(c0, width):
        return jnp.dot(h_ref[...], w_ref[:, c0:c0 + width], preferred_element_type=F32)

    def rope_cols(t, kind):
        if kind == "main":
            cos, sin, period, half = cm_ref[0], sm_ref[0], HEAD_DIM, ROT_DIM // 2
        else:
            cos, sin, period, half = ci_ref[0], si_ref[0], IDX_DIM, IDX_ROT_DIM // 2
        slabs = [_rope(t[:, j:j + LANES], cos, sin, period, half) for j in range(0, t.shape[1], LANES)]
        return slabs[0] if len(slabs) == 1 else jnp.concatenate(slabs, axis=1)

    col = 0
    plan = [(qa_ref, 1024, "main"), (ka_ref, 1024, "main"), (va_ref, 1024, None), (qb_ref, 1024, "main"),
            (kb_ref, 256, "main"), (vb_ref, 256, None), (qi_ref, 512, "idx"),
            (klo_ref, 128, "idx"), (khi_ref, 128, "idx"), (wi_ref, 128, "wi"),
            (ga_ref, 1024, "gate"), (gb_ref, 1024, "gate")]
    for ref, width, kind in plan:
        for c0 in range(0, width, _IN_TILE):
            wd = min(_IN_TILE, width - c0)
            t = proj(col + c0, wd)
            if kind in ("main", "idx"):
                t = rope_cols(t, kind)
            elif kind == "wi":
                t = t * (IDX_HEADS ** -0.5)
            elif kind == "gate":
                t = jax.nn.sigmoid(t)
            ref[0, :, c0:c0 + wd] = t.astype(ref.dtype)
        col += width


def _arrange_w_in(w_in_l):
    d = w_in_l.shape[0]
    ki = w_in_l[:, _OFF_KI:_OFF_KI + IDX_DIM]
    z64 = jnp.zeros((d, IDX_DIM), w_in_l.dtype)
    wi = w_in_l[:, _OFF_WI:_OFF_WI + IDX_HEADS]
    zwi = jnp.zeros((d, LANES - IDX_HEADS), w_in_l.dtype)
    parts = [w_in_l[:, :_OFF_KI], ki, z64, z64, ki, wi, zwi, w_in_l[:, _OFF_GA:]]
    return jnp.concatenate(parts, axis=1).astype(BF16)


def _inproj(x, sc, sh, g, w_arr, tabs, tm):
    b, s, d = x.shape
    n = w_arr.shape[1]
    row = lambda width: pl.BlockSpec((1, tm, width), lambda bi, i: (bi, i, 0))
    mod = pl.BlockSpec((1, 1, d), lambda bi, i: (bi, 0, 0))
    outs = [("qa", 1024, BF16), ("ka", 1024, BF16), ("va", 1024, BF16), ("qb", 1024, BF16),
            ("kb", 256, BF16), ("vb", 256, BF16), ("qi", 512, BF16), ("klo", 128, BF16), ("khi", 128, BF16),
            ("wi", 128, F32), ("ga", 1024, F32), ("gb", 1024, F32)]
    res = pl.pallas_call(
        _inproj_kernel,
        grid=(b, s // tm),
        in_specs=[row(d), mod, mod, pl.BlockSpec((1, d), lambda bi, i: (0, 0)),
                  _resident((d, n), lambda bi, i: (0, 0)),
                  row(LANES), row(LANES), row(LANES), row(LANES)],
        out_specs=[row(wd) for _, wd, _ in outs],
        out_shape=[jax.ShapeDtypeStruct((b, s, wd), dt) for _, wd, dt in outs],
        scratch_shapes=[pltpu.VMEM((tm, d), BF16)],
        compiler_params=_params(2),
        name="in_proj",
    )(x, sc, sh, g.reshape(1, d), w_arr, *tabs)
    return dict(zip([nm for nm, _, _ in outs], res))


def _moba_kernel(q_ref, k_ref, v_ref, o_ref, kmean_ref, m_ref, l_ref, acc_ref, *, nb):
    i = pl.program_id(2)
    tq = MOBA_BLOCK
    scale = HEAD_DIM ** -0.5

    @pl.when(i == 0)
    def _():
        kmean_ref[...] = jnp.zeros_like(kmean_ref)
        for j in range(nb):
            kb = k_ref[0, j * tq:(j + 1) * tq, :].astype(F32)
            kmean_ref[j:j + 1, :] = (jnp.sum(kb, axis=0, keepdims=True) * (1.0 / tq)).astype(BF16)

    q = q_ref[0]
    nt = (((1,), (1,)), ((), ()))
    gate = lax.dot_general(q, kmean_ref[...], nt, preferred_element_type=F32)
    lane = lax.broadcasted_iota(I32, gate.shape, 1)
    gate = jnp.where(lane < i, gate, NEG_INF)
    selected = jnp.zeros(gate.shape, jnp.bool_)
    for _ in range(MOBA_TOPK):
        mx = jnp.max(gate, axis=-1, keepdims=True)
        first = jnp.min(jnp.where(gate == mx, lane, LANES), axis=-1, keepdims=True)
        hit = lane == first
        selected = jnp.logical_or(selected, hit)
        gate = jnp.where(hit, -jnp.inf, gate)
    selbias = jnp.where(jnp.logical_and(selected, lane < i), 0.0, -jnp.inf)

    k_own = k_ref[0, pl.ds(pl.multiple_of(i * tq, tq), tq), :]
    v_own = v_ref[0, pl.ds(pl.multiple_of(i * tq, tq), tq), :]
    s = lax.dot_general(q, k_own, nt, preferred_element_type=F32) * scale
    r = lax.broadcasted_iota(I32, s.shape, 0)
    c = lax.broadcasted_iota(I32, s.shape, 1)
    s = jnp.where(c <= r, s, -jnp.inf)
    m0 = jnp.max(s, axis=-1, keepdims=True)
    p = jnp.exp(s - m0)
    m_ref[...] = m0
    l_ref[...] = jnp.sum(p, axis=-1, keepdims=True)
    acc_ref[...] = jnp.dot(p.astype(BF16), v_own, preferred_element_type=F32)

    def body(j, carry):
        kj = k_ref[0, pl.ds(pl.multiple_of(j * tq, tq), tq), :]
        vj = v_ref[0, pl.ds(pl.multiple_of(j * tq, tq), tq), :]
        bias = jnp.sum(jnp.where(lane == j, selbias, 0.0), axis=-1, keepdims=True)
        sj = lax.dot_general(q, kj, nt, preferred_element_type=F32) * scale + bias
        m_old = m_ref[...]
        m_new = jnp.maximum(m_old, jnp.max(sj, axis=-1, keepdims=True))
        alpha = jnp.exp(m_old - m_new)
        pj = jnp.exp(sj - m_new)
        l_ref[...] = alpha * l_ref[...] + jnp.sum(pj, axis=-1, keepdims=True)
        acc_ref[...] = alpha * acc_ref[...] + jnp.dot(pj.astype(BF16), vj, preferred_element_type=F32)
        m_ref[...] = m_new
        return carry

    lax.fori_loop(0, i, body, 0)
    o_ref[0] = acc_ref[...] / l_ref[...]


def _moba(qa, ka, va):
    b, s, _ = qa.shape
    tq = MOBA_BLOCK
    nb = s // tq
    assert s % tq == 0 and nb <= LANES
    return pl.pallas_call(
        functools.partial(_moba_kernel, nb=nb),
        grid=(b, MOBA_HEADS, nb),
        in_specs=[pl.BlockSpec((1, tq, HEAD_DIM), lambda bi, h, i: (bi, i, h)),
                  pl.BlockSpec((1, s, HEAD_DIM), lambda bi, h, i: (bi, 0, h)),
                  pl.BlockSpec((1, s, HEAD_DIM), lambda bi, h, i: (bi, 0, h))],
        out_specs=pl.BlockSpec((1, tq, HEAD_DIM), lambda bi, h, i: (bi, i, h)),
        out_shape=jax.ShapeDtypeStruct((b, s, MOBA_HEADS * HEAD_DIM), F32),
        scratch_shapes=[pltpu.VMEM((LANES, HEAD_DIM), BF16),
                        pltpu.VMEM((tq, 1), F32), pltpu.VMEM((tq, 1), F32), pltpu.VMEM((tq, HEAD_DIM), F32)],
        compiler_params=_params(3),
        name="moba_attn",
    )(qa, ka, va)


_DSA_TQ = 128
_DSA_TK = 512


def _sortable(x):
    bits = lax.bitcast_convert_type(x, I32)
    return bits ^ ((bits >> 31) & INT_MAX)


def _dsa_kernel(qb_ref, kb_ref, vb_ref, qi_ref, klo_ref, khi_ref, wi_ref, o_ref,
                key_ref, p_ref, m_ref, l_ref, acc_ref, *, n_top):
    i = pl.program_id(1)
    tq, tk = _DSA_TQ, _DSA_TK
    n_chunks = (i * tq + tq + tk - 1) // tk
    nt = (((1,), (1,)), ((), ()))
    scale = HEAD_DIM ** -0.5
    key_neg = int(np.array(NEG_INF, np.float32).view(np.int32)) ^ INT_MAX

    row = lax.broadcasted_iota(I32, (tq, tk), 0)
    colk = lax.broadcasted_iota(I32, (tq, tk), 1)
    qpos = i * tq + row

    wi = wi_ref[0] * (IDX_DIM ** -0.5)
    wcol = [wi[:, h:h + 1] for h in range(IDX_HEADS)]
    qi4 = jnp.concatenate([qi_ref[0, :, p * LANES:(p + 1) * LANES] for p in range(IDX_HEADS // 2)], axis=0)

    def idx_body(c, carry):
        off = pl.multiple_of(c * tk, tk)
        le = lax.dot_general(qi4, klo_ref[0, pl.ds(off, tk), :], nt, preferred_element_type=F32)
        lo = lax.dot_general(qi4, khi_ref[0, pl.ds(off, tk), :], nt, preferred_element_type=F32)
        sc = jnp.zeros((tq, tk), F32)
        for p in range(IDX_HEADS // 2):
            sc = sc + wcol[2 * p] * jnp.maximum(le[p * tq:(p + 1) * tq], 0.0)
            sc = sc + wcol[2 * p + 1] * jnp.maximum(lo[p * tq:(p + 1) * tq], 0.0)
        sc = jnp.where(c * tk + colk <= qpos, sc, NEG_INF)
        key_ref[c] = _sortable(sc)
        return carry

    lax.fori_loop(0, n_chunks, idx_body, 0)

    lane = lax.broadcasted_iota(I32, (tq, LANES), 1)

    def count(pred):
        def body(c, acc):
            for j in range(tk // LANES):
                ks = key_ref[c, :, j * LANES:(j + 1) * LANES]
                acc = acc + jnp.where(pred(ks, c * tk + j * LANES + lane), 1, 0)
            return acc
        acc = lax.fori_loop(0, n_chunks, body, jnp.zeros((tq, LANES), I32))
        return jnp.sum(acc, axis=-1, keepdims=True)

    def count_ge(cand):
        cb = jnp.broadcast_to(cand, (tq, LANES))
        return count(lambda ks, kp: ks >= cb)

    zero = jnp.zeros((tq, 1), I32)
    c0 = count_ge(zero)
    nonneg = c0 >= n_top
    thr = jnp.where(nonneg, 0, INT_MIN)
    cnt = jnp.where(nonneg, c0, n_chunks * tk)

    def bit_body(b, carry):
        thr, cnt = carry
        cand = thr | lax.shift_left(jnp.int32(1), 30 - b)
        cc = count_ge(cand)
        ok = cc >= n_top
        return jnp.where(ok, cand, thr), jnp.where(ok, cc, cnt)

    thr, cnt = lax.fori_loop(0, 31, bit_body, (thr, cnt))

    p_ref[...] = jnp.full((tq, 1), INT_MAX, I32)
    tie = jnp.logical_and(cnt > n_top, thr > key_neg)
    any_tie = jnp.max(jnp.where(tie, 1, 0)) > 0

    @pl.when(any_tie)
    def _():
        tb = jnp.broadcast_to(thr, (tq, LANES))
        need = n_top - count(lambda ks, kp: ks > tb)

        def pos_body(b, pos):
            cand = pos | lax.shift_left(jnp.int32(1), 30 - b)
            cb = jnp.broadcast_to(cand, (tq, LANES))
            below = count(lambda ks, kp: jnp.logical_and(ks == tb, kp < cb))
            return jnp.where(below < need, cand, pos)

        pos = lax.fori_loop(0, 31, pos_body, zero)
        p_ref[...] = jnp.where(tie, pos, INT_MAX)

    m_ref[...] = jnp.full(m_ref.shape, NEG_INF, F32)
    l_ref[...] = jnp.zeros(l_ref.shape, F32)
    acc_ref[...] = jnp.zeros(acc_ref.shape, F32)
    thr_b = jnp.broadcast_to(thr, (tq, tk))

    def att_body(c, carry):
        off = pl.multiple_of(c * tk, tk)
        kt = key_ref[c]
        kpos = c * tk + colk
        pb = jnp.broadcast_to(p_ref[...], (tq, tk))
        sel = jnp.logical_or(kt > thr_b, jnp.logical_and(kt == thr_b, kpos <= pb))
        sel = jnp.logical_and(sel, kpos <= qpos)
        bias = jnp.where(sel, 0.0, -jnp.inf)
        for g in range(DSA_KV_HEADS):
            kc = kb_ref[0, pl.ds(off, tk), g * HEAD_DIM:(g + 1) * HEAD_DIM]
            vc = vb_ref[0, pl.ds(off, tk), g * HEAD_DIM:(g + 1) * HEAD_DIM]
            ps = []
            for r in range(DSA_GROUP):
                h = g * DSA_GROUP + r
                q = qb_ref[0, :, h * HEAD_DIM:(h + 1) * HEAD_DIM]
                s = lax.dot_general(q, kc, nt, preferred_element_type=F32) * scale + bias
                m_old = m_ref[h]
                m_new = jnp.maximum(m_old, jnp.max(s, axis=-1, keepdims=True))
                alpha = jnp.exp(m_old - m_new)
                p = jnp.exp(s - m_new)
                l_ref[h] = alpha * l_ref[h] + jnp.sum(p, axis=-1, keepdims=True)
                acc_ref[h] = alpha * acc_ref[h]
                m_ref[h] = m_new
                ps.append(p.astype(BF16))
            pv = jnp.dot(jnp.concatenate(ps, axis=0), vc, preferred_element_type=F32)
            for r in range(DSA_GROUP):
                h = g * DSA_GROUP + r
                acc_ref[h] = acc_ref[h] + pv[r * tq:(r + 1) * tq]
        return carry

    lax.fori_loop(0, n_chunks, att_body, 0)
    for h in range(DSA_HEADS):
        o_ref[0, :, h * HEAD_DIM:(h + 1) * HEAD_DIM] = acc_ref[h] / l_ref[h]


def _dsa(qb, kb, vb, qi, klo, khi, wi):
    b, s, _ = qb.shape
    tq, tk = _DSA_TQ, _DSA_TK
    assert s % tk == 0
    n_top = min(DSA_TOPK_MAX, s // 4)
    assert tk >= n_top
    qrow = lambda width: pl.BlockSpec((1, tq, width), lambda bi, i: (bi, i, 0))
    full = lambda width: pl.BlockSpec((1, s, width), lambda bi, i: (bi, 0, 0))
    return pl.pallas_call(
        functools.partial(_dsa_kernel, n_top=n_top),
        grid=(b, s // tq),
        in_specs=[qrow(DSA_HEADS * HEAD_DIM), full(DSA_KV_HEADS * HEAD_DIM), full(DSA_KV_HEADS * HEAD_DIM),
                  qrow(IDX_HEADS * IDX_DIM), full(LANES), full(LANES), qrow(LANES)],
        out_specs=qrow(DSA_HEADS * HEAD_DIM),
        out_shape=jax.ShapeDtypeStruct((b, s, DSA_HEADS * HEAD_DIM), F32),
        scratch_shapes=[pltpu.VMEM((s // tk, tq, tk), I32),
                        pltpu.VMEM((tq, 1), I32),
                        pltpu.VMEM((DSA_HEADS, tq, 1), F32), pltpu.VMEM((DSA_HEADS, tq, 1), F32),
                        pltpu.VMEM((DSA_HEADS, tq, HEAD_DIM), F32)],
        compiler_params=_params(2),
        name="dsa_attn",
    )(qb, kb, vb, qi, klo, khi, wi)


def _rms(z, g):
    ms = jnp.mean(z * z, axis=-1, keepdims=True)
    return z * lax.rsqrt(ms + NORM_EPS) * g


def _outproj_kernel(ya_ref, yb_ref, ga_ref, gb_ref, x_ref, gt_ref, g_ref, w_ref, o_ref):
    mixed = ga_ref[0] * ya_ref[0] + gb_ref[0] * yb_ref[0]
    z = jnp.dot(mixed.astype(BF16), w_ref[...], preferred_element_type=F32)
    o_ref[0] = x_ref[0] + gt_ref[0] * _rms(z, g_ref[...])


def _outproj(ya, yb, ga, gb, x, gt, g, w, tm):
    b, s, d = x.shape
    row = pl.BlockSpec((1, tm, d), lambda bi, i: (bi, i, 0))
    return pl.pallas_call(
        _outproj_kernel,
        grid=(b, s // tm),
        in_specs=[row, row, row, row, row, pl.BlockSpec((1, 1, d), lambda bi, i: (bi, 0, 0)),
                  pl.BlockSpec((1, d), lambda bi, i: (0, 0)), _resident((d, d), lambda bi, i: (0, 0))],
        out_specs=row,
        out_shape=jax.ShapeDtypeStruct((b, s, d), F32),
        compiler_params=_params(2),
        name="out_proj",
    )(ya, yb, ga, gb, x, gt, g.reshape(1, d), w)


_FF_TILE = 256


def _ffn_kernel(x_ref, sc_ref, sh_ref, gt_ref, gpre_ref, gpost_ref, wg_ref, wu_ref, wd_ref, o_ref, a_ref):
    x = x_ref[0]
    h = (_rms(x, gpre_ref[...]) * (1.0 + sc_ref[0]) + sh_ref[0]).astype(BF16)
    for c0 in range(0, D_FF, _FF_TILE):
        gte = jnp.dot(h, wg_ref[:, c0:c0 + _FF_TILE], preferred_element_type=F32)
        up = jnp.dot(h, wu_ref[:, c0:c0 + _FF_TILE], preferred_element_type=F32)
        a_ref[:, c0:c0 + _FF_TILE] = (gte * jax.nn.sigmoid(gte) * up).astype(BF16)
    f = jnp.dot(a_ref[...], wd_ref[...], preferred_element_type=F32)
    o_ref[0] = x + gt_ref[0] * _rms(f, gpost_ref[...])


def _ffn(x, sc, sh, gt, gpre, gpost, wg, wu, wd, tm):
    b, s, d = x.shape
    ff = wg.shape[1]
    row = pl.BlockSpec((1, tm, d), lambda bi, i: (bi, i, 0))
    mod = pl.BlockSpec((1, 1, d), lambda bi, i: (bi, 0, 0))
    gain = pl.BlockSpec((1, d), lambda bi, i: (0, 0))
    return pl.pallas_call(
        _ffn_kernel,
        grid=(b, s // tm),
        in_specs=[row, mod, mod, mod, gain, gain,
                  _resident((d, ff), lambda bi, i: (0, 0)), _resident((d, ff), lambda bi, i: (0, 0)),
                  _resident((ff, d), lambda bi, i: (0, 0))],
        out_specs=row,
        out_shape=jax.ShapeDtypeStruct((b, s, d), F32),
        scratch_shapes=[pltpu.VMEM((tm, ff), BF16)],
        compiler_params=_params(2),
        name="ffn",
    )(x, sc, sh, gt, gpre.reshape(1, d), gpost.reshape(1, d), wg, wu, wd)


def kernel(x, c, positions, w_ada, b_ada, g_pre_mix, g_post_mix, w_in, w_out, g_pre_ffn, g_post_ffn,
           w_gate, w_up, w_down):
    b, s, d = x.shape
    depth = w_ada.shape[0]
    tm = min(s, 256)

    c_pad = jnp.zeros((8, d), F32).at[:b].set(c)
    mod = _ada(c_pad, w_ada, b_ada)[:, :b]
    tabs = _rope_tables(positions)

    for l in range(depth):
        sh1, sc1, gt1, sh2, sc2, gt2 = [mod[l, :, j * d:(j + 1) * d].reshape(b, 1, d) for j in range(N_MOD)]
        pr = _inproj(x, sc1, sh1, g_pre_mix[l], _arrange_w_in(w_in[l]), tabs, tm)
        ya = _moba(pr["qa"], pr["ka"], pr["va"])
        yb = _dsa(pr["qb"], pr["kb"], pr["vb"], pr["qi"], pr["klo"], pr["khi"], pr["wi"])
        x = _outproj(ya, yb, pr["ga"], pr["gb"], x, gt1, g_post_mix[l], w_out[l].astype(BF16), tm)
        x = _ffn(x, sc2, sh2, gt2, g_pre_ffn[l], g_post_ffn[l],
                 w_gate[l].astype(BF16), w_up[l].astype(BF16), w_down[l].astype(BF16), tm)
    return x
```

```python
import functools

import jax
import jax.numpy as jnp
import numpy as np
from jax import lax
from jax.experimental import pallas as pl
from jax.experimental.pallas import tpu as pltpu

F32 = jnp.float32
BF16 = jnp.bfloat16
I32 = jnp.int32

D_MODEL = 1024
HEAD_DIM = 128
ROT_DIM = HEAD_DIM // 4
ROPE_THETA = 500000.0
MOBA_HEADS = 8
MOBA_BLOCK = 256
MOBA_TOPK = 3
DSA_HEADS = 8
DSA_KV_HEADS = 2
DSA_GROUP = DSA_HEADS // DSA_KV_HEADS
DSA_TOPK_MAX = 256
IDX_HEADS = 8
IDX_DIM = 64
IDX_ROT_DIM = IDX_DIM // 4
D_FF = 2816
N_MOD = 6
NORM_EPS = 1e-6
NEG_INF = -1e30

LANES = 128
VMEM_LIMIT = 56 * 1024 * 1024

_OFF_QA, _OFF_KA, _OFF_VA, _OFF_QB = 0, 1024, 2048, 3072
_OFF_KB, _OFF_VB, _OFF_QI, _OFF_KI, _OFF_WI, _OFF_GA, _OFF_GB = 4096, 4352, 4608, 5120, 5184, 5192, 6216

INT_MIN = -2147483648
INT_MAX = 2147483647


def _params(n_axes):
    return pltpu.CompilerParams(dimension_semantics=("arbitrary",) * n_axes,
                                vmem_limit_bytes=VMEM_LIMIT)


def _resident(block_shape, index_map):
    return pl.BlockSpec(block_shape, index_map, pipeline_mode=pl.Buffered(1))


def _ada_kernel(c_ref, w_ref, b_ref, o_ref):
    c = c_ref[...]
    s = c * jax.nn.sigmoid(c)
    o_ref[0] = jnp.dot(s, w_ref[0], preferred_element_type=F32) + b_ref[0]


def _ada(c_pad, w_ada, b_ada):
    depth, d, n = w_ada.shape
    tn = 1536
    rows = c_pad.shape[0]
    return pl.pallas_call(
        _ada_kernel,
        grid=(depth, n // tn),
        in_specs=[pl.BlockSpec((rows, d), lambda l, j: (0, 0)),
                  pl.BlockSpec((1, d, tn), lambda l, j: (l, 0, j)),
                  pl.BlockSpec((1, 1, tn), lambda l, j: (l, 0, j))],
        out_specs=pl.BlockSpec((1, rows, tn), lambda l, j: (l, 0, j)),
        out_shape=jax.ShapeDtypeStruct((depth, rows, n), F32),
        compiler_params=_params(2),
        name="ada_mod",
    )(c_pad, w_ada, b_ada.reshape(depth, 1, n))


def _rope_table_kernel(pos_ref, inv_ref, sgn_ref, cm_ref, sm_ref, ci_ref, si_ref):
    pos = pos_ref[0].astype(F32)
    ang_m = pos * inv_ref[0:1, :]
    ang_i = pos * inv_ref[1:2, :]
    cm_ref[0] = jnp.cos(ang_m)
    sm_ref[0] = jnp.sin(ang_m) * sgn_ref[0:1, :]
    ci_ref[0] = jnp.cos(ang_i)
    si_ref[0] = jnp.sin(ang_i) * sgn_ref[1:2, :]


def _rope_lane_constants():
    lane = np.arange(LANES)
    half_m, half_i = ROT_DIM // 2, IDX_ROT_DIM // 2
    inv_m = ROPE_THETA ** (-jnp.arange(half_m, dtype=F32) * 2.0 / ROT_DIM)
    inv_i = ROPE_THETA ** (-jnp.arange(half_i, dtype=F32) * 2.0 / IDX_ROT_DIM)
    in_m = lane < ROT_DIM
    in_i = (lane % IDX_DIM) < IDX_ROT_DIM
    inv = jnp.stack([jnp.where(in_m, inv_m[lane % half_m], 0.0),
                     jnp.where(in_i, inv_i[lane % half_i], 0.0)]).astype(F32)
    sgn_m = np.where(lane < half_m, -1.0, 1.0)
    sgn_i = np.where((lane % IDX_DIM) < half_i, -1.0, 1.0)
    sgn = jnp.asarray(np.stack([sgn_m, sgn_i]), F32)
    return inv, sgn


def _rope_tables(positions):
    b, s = positions.shape
    tm = min(s, 1024)
    inv, sgn = _rope_lane_constants()
    tab = jax.ShapeDtypeStruct((b, s, LANES), F32)
    tspec = pl.BlockSpec((1, tm, LANES), lambda bi, i: (bi, i, 0))
    cspec = pl.BlockSpec((2, LANES), lambda bi, i: (0, 0))
    return pl.pallas_call(
        _rope_table_kernel,
        grid=(b, s // tm),
        in_specs=[pl.BlockSpec((1, tm, 1), lambda bi, i: (bi, i, 0)), cspec, cspec],
        out_specs=[tspec] * 4,
        out_shape=[tab] * 4,
        compiler_params=_params(2),
        name="rope_tables",
    )(positions.reshape(b, s, 1), inv, sgn)


def _rope(t, cos, sin, period, half):
    lane = lax.broadcasted_iota(I32, t.shape, 1)
    first = (lane % period) < half
    partner = jnp.where(first, pltpu.roll(t, LANES - half, 1), pltpu.roll(t, half, 1))
    return t * cos + partner * sin


_IN_TILE = 256


def _inproj_kernel(x_ref, sc_ref, sh_ref, g_ref, w_ref, cm_ref, sm_ref, ci_ref, si_ref,
                   qa_ref, ka_ref, va_ref, qb_ref, kb_ref, vb_ref, qi_ref, klo_ref, khi_ref,
                   wi_ref, ga_ref, gb_ref, h_ref):
    x = x_ref[0]
    ms = jnp.mean(x * x, axis=-1, keepdims=True)
    y = x * lax.rsqrt(ms + NORM_EPS) * g_ref[...]
    h_ref[...] = (y * (1.0 + sc_ref[0]) + sh_ref[0]).astype(BF16)

    def proj(c0, width):
        return jnp.dot(h_ref[...], w_ref[:, c0:c0 + width], preferred_element_type=F32)

    def rope_cols(t, kind):
        if kind == "main":
            cos, sin, period, half = cm_ref[0], sm_ref[0], HEAD_DIM, ROT_DIM // 2
        else:
            cos, sin, period, half = ci_ref[0], si_ref[0], IDX_DIM, IDX_ROT_DIM // 2
        slabs = [_rope(t[:, j:j + LANES], cos, sin, period, half) for j in range(0, t.shape[1], LANES)]
        return slabs[0] if len(slabs) == 1 else jnp.concatenate(slabs, axis=1)

    col = 0
    plan = [(qa_ref, 1024, "main"), (ka_ref, 1024, "main"), (va_ref, 1024, None), (qb_ref, 1024, "main"),
            (kb_ref, 256, "main"), (vb_ref, 256, None), (qi_ref, 512, "idx"),
            (klo_ref, 128, "idx"), (khi_ref, 128, "idx"), (wi_ref, 128, "wi"),
            (ga_ref, 1024, "gate"), (gb_ref, 1024, "gate")]
    for ref, width, kind in plan:
        for c0 in range(0, width, _IN_TILE):
            wd = min(_IN_TILE, width - c0)
            t = proj(col + c0, wd)
            if kind in ("main", "idx"):
                t = rope_cols(t, kind)
            elif kind == "wi":
                t = t * (IDX_HEADS ** -0.5)
            elif kind == "gate":
                t = jax.nn.sigmoid(t)
            ref[0, :, c0:c0 + wd] = t.astype(ref.dtype)
        col += width


def _arrange_w_in(w_in_l):
    d = w_in_l.shape[0]
    ki = w_in_l[:, _OFF_KI:_OFF_KI + IDX_DIM]
    z64 = jnp.zeros((d, IDX_DIM), w_in_l.dtype)
    wi = w_in_l[:, _OFF_WI:_OFF_WI + IDX_HEADS]
    zwi = jnp.zeros((d, LANES - IDX_HEADS), w_in_l.dtype)
    parts = [w_in_l[:, :_OFF_KI], ki, z64, z64, ki, wi, zwi, w_in_l[:, _OFF_GA:]]
    return jnp.concatenate(parts, axis=1).astype(BF16)


def _inproj(x, sc, sh, g, w_arr, tabs, tm):
    b, s, d = x.shape
    n = w_arr.shape[1]
    row = lambda width: pl.BlockSpec((1, tm, width), lambda bi, i: (bi, i, 0))
    mod = pl.BlockSpec((1, 1, d), lambda bi, i: (bi, 0, 0))
    outs = [("qa", 1024, BF16), ("ka", 1024, BF16), ("va", 1024, BF16), ("qb", 1024, BF16),
            ("kb", 256, BF16), ("vb", 256, BF16), ("qi", 512, BF16), ("klo", 128, BF16), ("khi", 128, BF16),
            ("wi", 128, F32), ("ga", 1024, F32), ("gb", 1024, F32)]
    res = pl.pallas_call(
        _inproj_kernel,
        grid=(b, s // tm),
        in_specs=[row(d), mod, mod, pl.BlockSpec((1, d), lambda bi, i: (0, 0)),
                  _resident((d, n), lambda bi, i: (0, 0)),
                  row(LANES), row(LANES), row(LANES), row(LANES)],
        out_specs=[row(wd) for _, wd, _ in outs],
        out_shape=[jax.ShapeDtypeStruct((b, s, wd), dt) for _, wd, dt in outs],
        scratch_shapes=[pltpu.VMEM((tm, d), BF16)],
        compiler_params=_params(2),
        name="in_proj",
    )(x, sc, sh, g.reshape(1, d), w_arr, *tabs)
    return dict(zip([nm for nm, _, _ in outs], res))


_NT = (((1,), (1,)), ((), ()))
_TN = (((0,), (0,)), ((), ()))
LOG2E = 1.4426950408889634
_MOBA_LOOKAHEAD = 2


def _moba_kernel(q_ref, k_ref, v_ref, o_ref, kmean_ref, bias_ref, m_ref, l_ref, acc_ref, *, nb, nbp):
    i = pl.program_id(1)
    tq = MOBA_BLOCK
    c_exp = (HEAD_DIM ** -0.5) * LOG2E

    @pl.when(i == 0)
    def _():
        kmean_ref[...] = jnp.zeros_like(kmean_ref)

        def mean_body(j, carry):
            kb = k_ref[0, pl.ds(pl.multiple_of(j * tq, tq), tq), :].astype(F32)
            kmean_ref[pl.ds(j, 1), :] = jnp.sum(kb, axis=0, keepdims=True) * (1.0 / tq)
            return carry

        lax.fori_loop(0, nb, mean_body, 0)

    off_own = pl.multiple_of(i * tq, tq)
    blk = lax.broadcasted_iota(I32, (nbp, tq), 0)
    kr = lax.broadcasted_iota(I32, (tq, tq), 0)
    qc = lax.broadcasted_iota(I32, (tq, tq), 1)
    for h in range(MOBA_HEADS):
        hs = slice(h * HEAD_DIM, (h + 1) * HEAD_DIM)
        q = q_ref[0, :, hs]
        km = kmean_ref[0:nbp, hs].astype(BF16)
        gate = lax.dot_general(km, q, _NT, preferred_element_type=F32)
        gate = jnp.where(blk < i, gate, NEG_INF)
        chosen = jnp.zeros(gate.shape, F32)
        for _ in range(MOBA_TOPK):
            mx = jnp.max(gate, axis=0, keepdims=True)
            first = jnp.min(jnp.where(gate == mx, blk, LANES), axis=0, keepdims=True)
            hit = blk == first
            chosen = jnp.where(hit, 1.0, chosen)
            gate = jnp.where(hit, -jnp.inf, gate)
        bias_ref[h] = jnp.where(blk < i, jnp.where(chosen > 0.0, 0.0, -jnp.inf), -jnp.inf)

        s = lax.dot_general(k_ref[0, pl.ds(off_own, tq), hs], q, _NT, preferred_element_type=F32) * c_exp
        s = jnp.where(kr <= qc, s, -jnp.inf)
        m0 = jnp.max(s, axis=0, keepdims=True)
        p = jnp.exp2(s - m0)
        m_ref[h] = m0
        l_ref[h] = jnp.sum(p, axis=0, keepdims=True)
        acc_ref[h] = lax.dot_general(v_ref[0, pl.ds(off_own, tq), hs], p.astype(BF16), _TN,
                                     preferred_element_type=F32)

    def body(j, carry):
        off = pl.multiple_of(j * tq, tq)

        def qk(h):
            hs = slice(h * HEAD_DIM, (h + 1) * HEAD_DIM)
            return lax.dot_general(k_ref[0, pl.ds(off, tq), hs], q_ref[0, :, hs], _NT, preferred_element_type=F32)

        pend = [qk(h) for h in range(_MOBA_LOOKAHEAD)]
        for h in range(MOBA_HEADS):
            hs = slice(h * HEAD_DIM, (h + 1) * HEAD_DIM)
            s = pend.pop(0)
            if h + _MOBA_LOOKAHEAD < MOBA_HEADS:
                pend.append(qk(h + _MOBA_LOOKAHEAD))
            s = s * c_exp + bias_ref[h, pl.ds(j, 1), :]
            m_old = m_ref[h]
            m_new = jnp.maximum(m_old, jnp.max(s, axis=0, keepdims=True))
            alpha = jnp.exp2(m_old - m_new)
            p = jnp.exp2(s - m_new)
            l_ref[h] = alpha * l_ref[h] + jnp.sum(p, axis=0, keepdims=True)
            acc_ref[h] = alpha * acc_ref[h] + lax.dot_general(
                v_ref[0, pl.ds(off, tq), hs], p.astype(BF16), _TN, preferred_element_type=F32)
            m_ref[h] = m_new
        return carry

    lax.fori_loop(0, i, body, 0)
    for h in range(MOBA_HEADS):
        o_ref[0, :, h * HEAD_DIM:(h + 1) * HEAD_DIM] = (acc_ref[h] / l_ref[h]).T


def _moba(qa, ka, va):
    b, s, w = qa.shape
    tq = MOBA_BLOCK
    nb = s // tq
    nbp = -(-nb // 8) * 8
    assert s % tq == 0 and nbp <= LANES
    return pl.pallas_call(
        functools.partial(_moba_kernel, nb=nb, nbp=nbp),
        grid=(b, nb),
        in_specs=[pl.BlockSpec((1, tq, w), lambda bi, i: (bi, i, 0)),
                  _resident((1, s, w), lambda bi, i: (bi, 0, 0)),
                  _resident((1, s, w), lambda bi, i: (bi, 0, 0))],
        out_specs=pl.BlockSpec((1, tq, w), lambda bi, i: (bi, i, 0)),
        out_shape=jax.ShapeDtypeStruct((b, s, w), F32),
        scratch_shapes=[pltpu.VMEM((LANES, w), F32),
                        pltpu.VMEM((MOBA_HEADS, nbp, tq), F32),
                        pltpu.VMEM((MOBA_HEADS, 1, tq), F32), pltpu.VMEM((MOBA_HEADS, 1, tq), F32),
                        pltpu.VMEM((MOBA_HEADS, HEAD_DIM, tq), F32)],
        compiler_params=_params(2),
        name="moba_attn",
    )(qa, ka, va)


_DSA_TQ = 128
_DSA_TK = 512
_DSA_SLAB = 64


def _ordered_to_f32(ci):
    return lax.bitcast_convert_type(ci ^ ((ci >> 31) & INT_MAX), F32)


def _dsa_kernel(qb_ref, kb_ref, vb_ref, qi_ref, klo_ref, khi_ref, wi_ref, o_ref,
                sc_ref, p_ref, m_ref, l_ref, acc_ref, *, n_top):
    i = pl.program_id(1)
    tq, tk, slab = _DSA_TQ, _DSA_TK, _DSA_SLAB
    n_chunks = (i * tq + tq + tk - 1) // tk
    c_exp = (HEAD_DIM ** -0.5) * LOG2E

    kr = lax.broadcasted_iota(I32, (tk, tq), 0)
    qpos = i * tq + lax.broadcasted_iota(I32, (tk, tq), 1)

    wt = (wi_ref[0] * (IDX_DIM ** -0.5)).T
    wrow = [wt[h:h + 1, :] for h in range(IDX_HEADS)]
    qi4 = jnp.concatenate([qi_ref[0, :, p * LANES:(p + 1) * LANES] for p in range(IDX_HEADS // 2)], axis=0)

    def idx_body(c, carry):
        off = pl.multiple_of(c * tk, tk)
        le = lax.dot_general(klo_ref[0, pl.ds(off, tk), :], qi4, _NT, preferred_element_type=F32)
        lo = lax.dot_general(khi_ref[0, pl.ds(off, tk), :], qi4, _NT, preferred_element_type=F32)
        sc = jnp.zeros((tk, tq), F32)
        for p in range(IDX_HEADS // 2):
            sc = sc + wrow[2 * p] * jnp.maximum(le[:, p * tq:(p + 1) * tq], 0.0)
            sc = sc + wrow[2 * p + 1] * jnp.maximum(lo[:, p * tq:(p + 1) * tq], 0.0)
        sc_ref[pl.ds(off, tk), :] = jnp.where(c * tk + kr <= qpos, sc, NEG_INF)
        return carry

    lax.fori_loop(0, n_chunks, idx_body, 0)

    srow = lax.broadcasted_iota(I32, (slab, tq), 0)

    def count(pred):
        def body(c, acc):
            for r in range(tk // slab):
                off = pl.multiple_of(c * tk + r * slab, slab)
                acc = acc + jnp.where(pred(sc_ref[pl.ds(off, slab), :], off + srow), 1, 0)
            return acc
        acc = lax.fori_loop(0, n_chunks, body, jnp.zeros((slab, tq), I32))
        return jnp.sum(acc, axis=0, keepdims=True)

    zero = jnp.zeros((1, tq), I32)
    c0 = count(lambda s, kp: s >= 0.0)
    nonneg = c0 >= n_top
    code = jnp.where(nonneg, 0, INT_MIN)
    cnt = jnp.where(nonneg, c0, n_chunks * tk)

    def bit_body(b, carry):
        code, cnt = carry
        cand = code | lax.shift_left(jnp.int32(1), 30 - b)
        cf = _ordered_to_f32(cand)
        cc = count(lambda s, kp: s >= cf)
        ok = cc >= n_top
        return jnp.where(ok, cand, code), jnp.where(ok, cc, cnt)

    code, cnt = lax.fori_loop(0, 31, bit_body, (code, cnt))
    thr = _ordered_to_f32(code)

    p_ref[...] = jnp.full((1, tq), INT_MAX, I32)
    tie = jnp.logical_and(cnt > n_top, thr > NEG_INF)
    any_tie = jnp.max(jnp.where(tie, 1, 0)) > 0

    @pl.when(any_tie)
    def _():
        need = n_top - count(lambda s, kp: s > thr)

        def pos_body(b, pos):
            cand = pos | lax.shift_left(jnp.int32(1), 30 - b)
            below = count(lambda s, kp: jnp.logical_and(s == thr, kp < cand))
            return jnp.where(below < need, cand, pos)

        pos = lax.fori_loop(0, 31, pos_body, zero)
        p_ref[...] = jnp.where(tie, pos, INT_MAX)

    m_ref[...] = jnp.full(m_ref.shape, NEG_INF, F32)
    l_ref[...] = jnp.zeros(l_ref.shape, F32)
    acc_ref[...] = jnp.zeros(acc_ref.shape, F32)
    last = p_ref[...]
    q4 = [jnp.concatenate([qb_ref[0, :, (g * DSA_GROUP + r) * HEAD_DIM:(g * DSA_GROUP + r + 1) * HEAD_DIM]
                           for r in range(DSA_GROUP)], axis=0) for g in range(DSA_KV_HEADS)]

    def att_body(c, carry):
        off = pl.multiple_of(c * tk, tk)
        s_all = [lax.dot_general(kb_ref[0, pl.ds(off, tk), g * HEAD_DIM:(g + 1) * HEAD_DIM], q4[g], _NT,
                                 preferred_element_type=F32) for g in range(DSA_KV_HEADS)]
        sc = sc_ref[pl.ds(off, tk), :]
        kp = c * tk + kr
        tie_bias = jnp.where(sc == thr, jnp.where(kp <= last, 0.0, -jnp.inf), -jnp.inf)
        bias = jnp.where(kp <= qpos, jnp.where(sc > thr, 0.0, tie_bias), -jnp.inf)
        bias4 = jnp.concatenate([bias] * DSA_GROUP, axis=1)
        for g in range(DSA_KV_HEADS):
            s = s_all[g] * c_exp + bias4
            m_old = m_ref[g]
            m_new = jnp.maximum(m_old, jnp.max(s, axis=0, keepdims=True))
            alpha = jnp.exp2(m_old - m_new)
            p = jnp.exp2(s - m_new)
            l_ref[g] = alpha * l_ref[g] + jnp.sum(p, axis=0, keepdims=True)
            acc_ref[g] = alpha * acc_ref[g] + lax.dot_general(
                vb_ref[0, pl.ds(off, tk), g * HEAD_DIM:(g + 1) * HEAD_DIM], p.astype(BF16), _TN,
                preferred_element_type=F32)
            m_ref[g] = m_new
        return carry

    lax.fori_loop(0, n_chunks, att_body, 0)
    for g in range(DSA_KV_HEADS):
        o_t = acc_ref[g] / l_ref[g]
        for r in range(DSA_GROUP):
            h = g * DSA_GROUP + r
            o_ref[0, :, h * HEAD_DIM:(h + 1) * HEAD_DIM] = o_t[:, r * tq:(r + 1) * tq].T


def _dsa(qb, kb, vb, qi, klo, khi, wi):
    b, s, _ = qb.shape
    tq, tk = _DSA_TQ, _DSA_TK
    assert s % tk == 0
    n_top = min(DSA_TOPK_MAX, s // 4)
    assert tk >= n_top
    qrow = lambda width: pl.BlockSpec((1, tq, width), lambda bi, i: (bi, i, 0))
    full = lambda width: _resident((1, s, width), lambda bi, i: (bi, 0, 0))
    return pl.pallas_call(
        functools.partial(_dsa_kernel, n_top=n_top),
        grid=(b, s // tq),
        in_specs=[qrow(DSA_HEADS * HEAD_DIM), full(DSA_KV_HEADS * HEAD_DIM), full(DSA_KV_HEADS * HEAD_DIM),
                  qrow(IDX_HEADS * IDX_DIM), full(LANES), full(LANES), qrow(LANES)],
        out_specs=qrow(DSA_HEADS * HEAD_DIM),
        out_shape=jax.ShapeDtypeStruct((b, s, DSA_HEADS * HEAD_DIM), F32),
        scratch_shapes=[pltpu.VMEM((s, tq), F32),
                        pltpu.VMEM((1, tq), I32),
                        pltpu.VMEM((DSA_KV_HEADS, 1, DSA_GROUP * tq), F32),
                        pltpu.VMEM((DSA_KV_HEADS, 1, DSA_GROUP * tq), F32),
                        pltpu.VMEM((DSA_KV_HEADS, HEAD_DIM, DSA_GROUP * tq), F32)],
        compiler_params=_params(2),
        name="dsa_attn",
    )(qb, kb, vb, qi, klo, khi, wi)


def _rms(z, g):
    ms = jnp.mean(z * z, axis=-1, keepdims=True)
    return z * lax.rsqrt(ms + NORM_EPS) * g


def _outproj_kernel(ya_ref, yb_ref, ga_ref, gb_ref, x_ref, gt_ref, g_ref, w_ref, o_ref):
    mixed = ga_ref[0] * ya_ref[0] + gb_ref[0] * yb_ref[0]
    z = jnp.dot(mixed.astype(BF16), w_ref[...], preferred_element_type=F32)
    o_ref[0] = x_ref[0] + gt_ref[0] * _rms(z, g_ref[...])


def _outproj(ya, yb, ga, gb, x, gt, g, w, tm):
    b, s, d = x.shape
    row = pl.BlockSpec((1, tm, d), lambda bi, i: (bi, i, 0))
    return pl.pallas_call(
        _outproj_kernel,
        grid=(b, s // tm),
        in_specs=[row, row, row, row, row, pl.BlockSpec((1, 1, d), lambda bi, i: (bi, 0, 0)),
                  pl.BlockSpec((1, d), lambda bi, i: (0, 0)), _resident((d, d), lambda bi, i: (0, 0))],
        out_specs=row,
        out_shape=jax.ShapeDtypeStruct((b, s, d), F32),
        compiler_params=_params(2),
        name="out_proj",
    )(ya, yb, ga, gb, x, gt, g.reshape(1, d), w)


_FF_TILE = 256


def _ffn_kernel(x_ref, sc_ref, sh_ref, gt_ref, gpre_ref, gpost_ref, wg_ref, wu_ref, wd_ref, o_ref, a_ref):
    x = x_ref[0]
    h = (_rms(x, gpre_ref[...]) * (1.0 + sc_ref[0]) + sh_ref[0]).astype(BF16)
    for c0 in range(0, D_FF, _FF_TILE):
        gte = jnp.dot(h, wg_ref[:, c0:c0 + _FF_TILE], preferred_element_type=F32)
        up = jnp.dot(h, wu_ref[:, c0:c0 + _FF_TILE], preferred_element_type=F32)
        a_ref[:, c0:c0 + _FF_TILE] = (gte * jax.nn.sigmoid(gte) * up).astype(BF16)
    f = jnp.dot(a_ref[...], wd_ref[...], preferred_element_type=F32)
    o_ref[0] = x + gt_ref[0] * _rms(f, gpost_ref[...])


def _ffn(x, sc, sh, gt, gpre, gpost, wg, wu, wd, tm):
    b, s, d = x.shape
    ff = wg.shape[1]
    row = pl.BlockSpec((1, tm, d), lambda bi, i: (bi, i, 0))
    mod = pl.BlockSpec((1, 1, d), lambda bi, i: (bi, 0, 0))
    gain = pl.BlockSpec((1, d), lambda bi, i: (0, 0))
    return pl.pallas_call(
        _ffn_kernel,
        grid=(b, s // tm),
        in_specs=[row, mod, mod, mod, gain, gain,
                  _resident((d, ff), lambda bi, i: (0, 0)), _resident((d, ff), lambda bi, i: (0, 0)),
                  _resident((ff, d), lambda bi, i: (0, 0))],
        out_specs=row,
        out_shape=jax.ShapeDtypeStruct((b, s, d), F32),
        scratch_shapes=[pltpu.VMEM((tm, ff), BF16)],
        compiler_params=_params(2),
        name="ffn",
    )(x, sc, sh, gt, gpre.reshape(1, d), gpost.reshape(1, d), wg, wu, wd)


def kernel(x, c, positions, w_ada, b_ada, g_pre_mix, g_post_mix, w_in, w_out, g_pre_ffn, g_post_ffn,
           w_gate, w_up, w_down):
    b, s, d = x.shape
    depth = w_ada.shape[0]
    tm = min(s, 256)

    c_pad = jnp.zeros((8, d), F32).at[:b].set(c)
    mod = _ada(c_pad, w_ada, b_ada)[:, :b]
    tabs = _rope_tables(positions)

    for l in range(depth):
        sh1, sc1, gt1, sh2, sc2, gt2 = [mod[l, :, j * d:(j + 1) * d].reshape(b, 1, d) for j in range(N_MOD)]
        pr = _inproj(x, sc1, sh1, g_pre_mix[l], _arrange_w_in(w_in[l]), tabs, tm)
        ya = _moba(pr["qa"], pr["ka"], pr["va"])
        yb = _dsa(pr["qb"], pr["kb"], pr["vb"], pr["qi"], pr["klo"], pr["khi"], pr["wi"])
        x = _outproj(ya, yb, pr["ga"], pr["gb"], x, gt1, g_post_mix[l], w_out[l].astype(BF16), tm)
        x = _ffn(x, sc2, sh2, gt2, g_pre_ffn[l], g_post_ffn[l],
                 w_gate[l].astype(BF16), w_up[l].astype(BF16), w_down[l].astype(BF16), tm)
    return x
```

```python
import functools

import jax
import jax.numpy as jnp
import numpy as np
from jax import lax
from jax.experimental import pallas as pl
from jax.experimental.pallas import tpu as pltpu

F32 = jnp.float32
BF16 = jnp.bfloat16
I32 = jnp.int32

D_MODEL = 1024
HEAD_DIM = 128
ROT_DIM = HEAD_DIM // 4
ROPE_THETA = 500000.0
MOBA_HEADS = 8
MOBA_BLOCK = 256
MOBA_TOPK = 3
DSA_HEADS = 8
DSA_KV_HEADS = 2
DSA_GROUP = DSA_HEADS // DSA_KV_HEADS
DSA_TOPK_MAX = 256
IDX_HEADS = 8
IDX_DIM = 64
IDX_ROT_DIM = IDX_DIM // 4
D_FF = 2816
N_MOD = 6
NORM_EPS = 1e-6
NEG_INF = -1e30

LANES = 128
VMEM_LIMIT = 56 * 1024 * 1024

_OFF_QA, _OFF_KA, _OFF_VA, _OFF_QB = 0, 1024, 2048, 3072
_OFF_KB, _OFF_VB, _OFF_QI, _OFF_KI, _OFF_WI, _OFF_GA, _OFF_GB = 4096, 4352, 4608, 5120, 5184, 5192, 6216

INT_MIN = -2147483648
INT_MAX = 2147483647


def _params(n_axes):
    return pltpu.CompilerParams(dimension_semantics=("arbitrary",) * n_axes,
                                vmem_limit_bytes=VMEM_LIMIT)


def _resident(block_shape, index_map):
    return pl.BlockSpec(block_shape, index_map, pipeline_mode=pl.Buffered(1))


def _ada_kernel(c_ref, w_ref, b_ref, o_ref):
    c = c_ref[...]
    s = c * jax.nn.sigmoid(c)
    o_ref[0] = jnp.dot(s, w_ref[0], preferred_element_type=F32) + b_ref[0]


def _ada(c_pad, w_ada, b_ada):
    depth, d, n = w_ada.shape
    tn = 1536
    rows = c_pad.shape[0]
    return pl.pallas_call(
        _ada_kernel,
        grid=(depth, n // tn),
        in_specs=[pl.BlockSpec((rows, d), lambda l, j: (0, 0)),
                  pl.BlockSpec((1, d, tn), lambda l, j: (l, 0, j)),
                  pl.BlockSpec((1, 1, tn), lambda l, j: (l, 0, j))],
        out_specs=pl.BlockSpec((1, rows, tn), lambda l, j: (l, 0, j)),
        out_shape=jax.ShapeDtypeStruct((depth, rows, n), F32),
        compiler_params=_params(2),
        name="ada_mod",
    )(c_pad, w_ada, b_ada.reshape(depth, 1, n))


def _rope_table_kernel(pos_ref, inv_ref, sgn_ref, cm_ref, sm_ref, ci_ref, si_ref):
    pos = pos_ref[0].astype(F32)
    ang_m = pos * inv_ref[0:1, :]
    ang_i = pos * inv_ref[1:2, :]
    cm_ref[0] = jnp.cos(ang_m)
    sm_ref[0] = jnp.sin(ang_m) * sgn_ref[0:1, :]
    ci_ref[0] = jnp.cos(ang_i)
    si_ref[0] = jnp.sin(ang_i) * sgn_ref[1:2, :]


def _rope_lane_constants():
    lane = np.arange(LANES)
    half_m, half_i = ROT_DIM // 2, IDX_ROT_DIM // 2
    inv_m = ROPE_THETA ** (-jnp.arange(half_m, dtype=F32) * 2.0 / ROT_DIM)
    inv_i = ROPE_THETA ** (-jnp.arange(half_i, dtype=F32) * 2.0 / IDX_ROT_DIM)
    in_m = lane < ROT_DIM
    in_i = (lane % IDX_DIM) < IDX_ROT_DIM
    inv = jnp.stack([jnp.where(in_m, inv_m[lane % half_m], 0.0),
                     jnp.where(in_i, inv_i[lane % half_i], 0.0)]).astype(F32)
    sgn_m = np.where(lane < half_m, -1.0, 1.0)
    sgn_i = np.where((lane % IDX_DIM) < half_i, -1.0, 1.0)
    sgn = jnp.asarray(np.stack([sgn_m, sgn_i]), F32)
    return inv, sgn


def _rope_tables(positions):
    b, s = positions.shape
    tm = min(s, 1024)
    inv, sgn = _rope_lane_constants()
    tab = jax.ShapeDtypeStruct((b, s, LANES), F32)
    tspec = pl.BlockSpec((1, tm, LANES), lambda bi, i: (bi, i, 0))
    cspec = pl.BlockSpec((2, LANES), lambda bi, i: (0, 0))
    return pl.pallas_call(
        _rope_table_kernel,
        grid=(b, s // tm),
        in_specs=[pl.BlockSpec((1, tm, 1), lambda bi, i: (bi, i, 0)), cspec, cspec],
        out_specs=[tspec] * 4,
        out_shape=[tab] * 4,
        compiler_params=_params(2),
        name="rope_tables",
    )(positions.reshape(b, s, 1), inv, sgn)


def _rope(t, cos, sin, period, half):
    lane = lax.broadcasted_iota(I32, t.shape, 1)
    first = (lane % period) < half
    partner = jnp.where(first, pltpu.roll(t, LANES - half, 1), pltpu.roll(t, half, 1))
    return t * cos + partner * sin


_IN_TILE = 256
QK_EXP2_SCALE = (HEAD_DIM ** -0.5) * 1.4426950408889634


def _inproj_kernel(x_ref, sc_ref, sh_ref, g_ref, w_ref, cm_ref, sm_ref, ci_ref, si_ref,
                   qa_ref, ka_ref, va_ref, qb_ref, kb_ref, vb_ref, qi_ref, klo_ref, khi_ref,
                   wi_ref, ga_ref, gb_ref, h_ref):
    x = x_ref[0]
    ms = jnp.mean(x * x, axis=-1, keepdims=True)
    y = x * lax.rsqrt(ms + NORM_EPS) * g_ref[...]
    h_ref[...] = (y * (1.0 + sc_ref[0]) + sh_ref[0]).astype(BF16)

    def proj(c0, width):
        return jnp.dot(h_ref[...], w_ref[:, c0:c0 + width], preferred_element_type=F32)

    def rope_cols(t, kind):
        if kind == "main":
            cos, sin, period, half = cm_ref[0], sm_ref[0], HEAD_DIM, ROT_DIM // 2
        else:
            cos, sin, period, half = ci_ref[0], si_ref[0], IDX_DIM, IDX_ROT_DIM // 2
        slabs = [_rope(t[:, j:j + LANES], cos, sin, period, half) for j in range(0, t.shape[1], LANES)]
        return slabs[0] if len(slabs) == 1 else jnp.concatenate(slabs, axis=1)

    col = 0
    plan = [(qa_ref, 1024, "query"), (ka_ref, 1024, "main"), (va_ref, 1024, None), (qb_ref, 1024, "query"),
            (kb_ref, 256, "main"), (vb_ref, 256, None), (qi_ref, 512, "idx"),
            (klo_ref, 128, "idx"), (khi_ref, 128, "idx"), (wi_ref, 128, "wi"),
            (ga_ref, 1024, "gate"), (gb_ref, 1024, "gate")]
    for ref, width, kind in plan:
        for c0 in range(0, width, _IN_TILE):
            wd = min(_IN_TILE, width - c0)
            t = proj(col + c0, wd)
            if kind in ("main", "idx"):
                t = rope_cols(t, kind)
            elif kind == "query":
                t = rope_cols(t, "main") * QK_EXP2_SCALE
            elif kind == "wi":
                t = t * (IDX_HEADS ** -0.5)
            elif kind == "gate":
                t = jax.nn.sigmoid(t)
            ref[0, :, c0:c0 + wd] = t.astype(ref.dtype)
        col += width


def _arrange_w_in(w_in_l):
    d = w_in_l.shape[0]
    ki = w_in_l[:, _OFF_KI:_OFF_KI + IDX_DIM]
    z64 = jnp.zeros((d, IDX_DIM), w_in_l.dtype)
    wi = w_in_l[:, _OFF_WI:_OFF_WI + IDX_HEADS]
    zwi = jnp.zeros((d, LANES - IDX_HEADS), w_in_l.dtype)
    parts = [w_in_l[:, :_OFF_KI], ki, z64, z64, ki, wi, zwi, w_in_l[:, _OFF_GA:]]
    return jnp.concatenate(parts, axis=1).astype(BF16)


def _inproj(x, sc, sh, g, w_arr, tabs, tm):
    b, s, d = x.shape
    n = w_arr.shape[1]
    row = lambda width: pl.BlockSpec((1, tm, width), lambda bi, i: (bi, i, 0))
    mod = pl.BlockSpec((1, 1, d), lambda bi, i: (bi, 0, 0))
    outs = [("qa", 1024, BF16), ("ka", 1024, BF16), ("va", 1024, BF16), ("qb", 1024, BF16),
            ("kb", 256, BF16), ("vb", 256, BF16), ("qi", 512, BF16), ("klo", 128, BF16), ("khi", 128, BF16),
            ("wi", 128, F32), ("ga", 1024, F32), ("gb", 1024, F32)]
    res = pl.pallas_call(
        _inproj_kernel,
        grid=(b, s // tm),
        in_specs=[row(d), mod, mod, pl.BlockSpec((1, d), lambda bi, i: (0, 0)),
                  _resident((d, n), lambda bi, i: (0, 0)),
                  row(LANES), row(LANES), row(LANES), row(LANES)],
        out_specs=[row(wd) for _, wd, _ in outs],
        out_shape=[jax.ShapeDtypeStruct((b, s, wd), dt) for _, wd, dt in outs],
        scratch_shapes=[pltpu.VMEM((tm, d), BF16)],
        compiler_params=_params(2),
        name="in_proj",
    )(x, sc, sh, g.reshape(1, d), w_arr, *tabs)
    return dict(zip([nm for nm, _, _ in outs], res))


_NT = (((1,), (1,)), ((), ()))
_TN = (((0,), (0,)), ((), ()))


def _moba_kernel(q_ref, k_ref, v_ref, o_ref, kmean_ref, bias_ref, m_ref, l_ref, acc_ref, s0_ref, s1_ref,
                 *, nb, nbp):
    i = pl.program_id(1)
    tq = MOBA_BLOCK

    @pl.when(i == 0)
    def _():
        kmean_ref[...] = jnp.zeros_like(kmean_ref)

        def mean_body(j, carry):
            kb = k_ref[0, pl.ds(pl.multiple_of(j * tq, tq), tq), :].astype(F32)
            kmean_ref[pl.ds(j, 1), :] = jnp.sum(kb, axis=0, keepdims=True) * (1.0 / tq)
            return carry

        lax.fori_loop(0, nb, mean_body, 0)

    off_own = pl.multiple_of(i * tq, tq)
    blk = lax.broadcasted_iota(I32, (nbp, tq), 0)
    kr = lax.broadcasted_iota(I32, (tq, tq), 0)
    qc = lax.broadcasted_iota(I32, (tq, tq), 1)
    for h in range(MOBA_HEADS):
        hs = slice(h * HEAD_DIM, (h + 1) * HEAD_DIM)
        q = q_ref[0, :, hs]
        km = kmean_ref[0:nbp, hs].astype(BF16)
        gate = lax.dot_general(km, q, _NT, preferred_element_type=F32)
        gate = jnp.where(blk < i, gate, NEG_INF)
        chosen = jnp.zeros(gate.shape, F32)
        for _ in range(MOBA_TOPK):
            mx = jnp.max(gate, axis=0, keepdims=True)
            first = jnp.min(jnp.where(gate == mx, blk, LANES), axis=0, keepdims=True)
            hit = blk == first
            chosen = jnp.where(hit, 1.0, chosen)
            gate = jnp.where(hit, -jnp.inf, gate)
        bias_ref[h] = jnp.where(blk < i, jnp.where(chosen > 0.0, 0.0, -jnp.inf), -jnp.inf)

        s = lax.dot_general(k_ref[0, pl.ds(off_own, tq), hs], q, _NT, preferred_element_type=F32)
        s = jnp.where(kr <= qc, s, -jnp.inf)
        m0 = jnp.max(s, axis=0, keepdims=True)
        p = jnp.exp2(s - m0)
        m_ref[h] = m0
        l_ref[h] = jnp.sum(p, axis=0, keepdims=True)
        acc_ref[h] = lax.dot_general(v_ref[0, pl.ds(off_own, tq), hs], p.astype(BF16), _TN,
                                     preferred_element_type=F32)

    def stage_a(j, s_ref):
        off = pl.multiple_of(j * tq, tq)
        for h in range(MOBA_HEADS):
            hs = slice(h * HEAD_DIM, (h + 1) * HEAD_DIM)
            s_ref[h] = lax.dot_general(k_ref[0, pl.ds(off, tq), hs], q_ref[0, :, hs], _NT,
                                       preferred_element_type=F32)

    def stage_b(j, s_ref):
        off = pl.multiple_of(j * tq, tq)
        for h in range(MOBA_HEADS):
            hs = slice(h * HEAD_DIM, (h + 1) * HEAD_DIM)
            s = s_ref[h]
            gate_row = bias_ref[h, pl.ds(j, 1), :]
            m_old = m_ref[h]
            m_new = jnp.maximum(m_old, jnp.max(s, axis=0, keepdims=True) + gate_row)
            alpha = jnp.exp2(m_old - m_new)
            p = jnp.exp2(s - (m_new - gate_row))
            l_ref[h] = alpha * l_ref[h] + jnp.sum(p, axis=0, keepdims=True)
            acc_ref[h] = alpha * acc_ref[h] + lax.dot_general(
                v_ref[0, pl.ds(off, tq), hs], p.astype(BF16), _TN, preferred_element_type=F32)
            m_ref[h] = m_new

    @pl.when(i > 0)
    def _():
        stage_a(0, s0_ref)

    def pair_body(t, carry):
        j = 2 * t
        stage_a(j + 1, s1_ref)
        stage_b(j, s0_ref)
        stage_a(j + 2, s0_ref)
        stage_b(j + 1, s1_ref)
        return carry

    lax.fori_loop(0, i // 2, pair_body, 0)

    @pl.when(i % 2 == 1)
    def _():
        stage_b(i - 1, s0_ref)

    for h in range(MOBA_HEADS):
        o_ref[0, :, h * HEAD_DIM:(h + 1) * HEAD_DIM] = (acc_ref[h] / l_ref[h]).T


def _moba(qa, ka, va):
    b, s, w = qa.shape
    tq = MOBA_BLOCK
    nb = s // tq
    nbp = -(-nb // 8) * 8
    assert s % tq == 0 and nbp <= LANES
    return pl.pallas_call(
        functools.partial(_moba_kernel, nb=nb, nbp=nbp),
        grid=(b, nb),
        in_specs=[pl.BlockSpec((1, tq, w), lambda bi, i: (bi, i, 0)),
                  _resident((1, s, w), lambda bi, i: (bi, 0, 0)),
                  _resident((1, s, w), lambda bi, i: (bi, 0, 0))],
        out_specs=pl.BlockSpec((1, tq, w), lambda bi, i: (bi, i, 0)),
        out_shape=jax.ShapeDtypeStruct((b, s, w), F32),
        scratch_shapes=[pltpu.VMEM((LANES, w), F32),
                        pltpu.VMEM((MOBA_HEADS, nbp, tq), F32),
                        pltpu.VMEM((MOBA_HEADS, 1, tq), F32), pltpu.VMEM((MOBA_HEADS, 1, tq), F32),
                        pltpu.VMEM((MOBA_HEADS, HEAD_DIM, tq), F32),
                        pltpu.VMEM((MOBA_HEADS, tq, tq), F32), pltpu.VMEM((MOBA_HEADS, tq, tq), F32)],
        compiler_params=_params(2),
        name="moba_attn",
    )(qa, ka, va)


_DSA_TQ = 128
_DSA_TK = 512
_DSA_SLAB = 64


def _dsa_kernel(qb_ref, kb_ref, vb_ref, qi_ref, klo_ref, khi_ref, wi_ref, o_ref,
                sc_ref, gmax_ref, m_ref, l_ref, acc_ref, s0_ref, s1_ref, *, n_top, pos_bits):
    i = pl.program_id(1)
    tq, tk, slab = _DSA_TQ, _DSA_TK, _DSA_SLAB
    n_chunks = (i * tq + tq + tk - 1) // tk
    f32_lowest = float(np.finfo(np.float32).min)

    kr = lax.broadcasted_iota(I32, (tk, tq), 0)
    qpos = i * tq + lax.broadcasted_iota(I32, (tk, tq), 1)

    wt = (wi_ref[0] * (IDX_DIM ** -0.5)).T
    wrow = [wt[h:h + 1, :] for h in range(IDX_HEADS)]
    pairs = IDX_HEADS // 2
    qi2 = [jnp.concatenate([qi_ref[0, :, p * LANES:(p + 1) * LANES] for p in (2 * u, 2 * u + 1)], axis=0)
           for u in range(pairs // 2)]
    gmax_ref[...] = jnp.full(gmax_ref.shape, -jnp.inf, F32)

    def idx_body(c, carry):
        off = pl.multiple_of(c * tk, tk)
        units = [(u, par) for u in range(pairs // 2) for par in (0, 1)]

        def logits(u, par):
            kref = klo_ref if par == 0 else khi_ref
            return lax.dot_general(kref[0, pl.ds(off, tk), :], qi2[u], _NT, preferred_element_type=F32)

        sc = jnp.zeros((tk, tq), F32)
        nxt = logits(*units[0])
        for n, (u, par) in enumerate(units):
            lg = nxt
            if n + 1 < len(units):
                nxt = logits(*units[n + 1])
            for j in range(2):
                h = 2 * (2 * u + j) + par
                sc = sc + wrow[h] * jnp.maximum(lg[:, j * tq:(j + 1) * tq], 0.0)
        sc = jnp.where(c * tk + kr <= qpos, sc, -jnp.inf)
        sc_ref[pl.ds(off, tk), :] = sc
        part = sc[0:n_top]
        for r in range(1, tk // n_top):
            part = jnp.maximum(part, sc[r * n_top:(r + 1) * n_top])
        gmax_ref[...] = jnp.maximum(gmax_ref[...], part)
        return carry

    lax.fori_loop(0, n_chunks, idx_body, 0)

    def code_of(x):
        bits = lax.bitcast_convert_type(x, I32)
        return bits ^ ((bits >> 31) & INT_MAX) ^ INT_MIN

    def float_of(u):
        ci = u ^ INT_MIN
        return lax.bitcast_convert_type(ci ^ ((ci >> 31) & INT_MAX), F32)

    srow = lax.broadcasted_iota(I32, (slab, tq), 0)

    def count(pred):
        def body(c, acc):
            for r in range(tk // slab):
                off = pl.multiple_of(c * tk + r * slab, slab)
                acc = acc + jnp.where(pred(sc_ref[pl.ds(off, slab), :], off + srow), 1, 0)
            return acc
        acc = lax.fori_loop(0, n_chunks, body, jnp.zeros((slab, tq), I32))
        return jnp.sum(acc, axis=0, keepdims=True)

    gm = gmax_ref[...]
    lo_b = jnp.maximum(jnp.min(gm, axis=0, keepdims=True), f32_lowest)
    hi_b = jnp.max(gm, axis=0, keepdims=True)
    u_lo, u_hi = code_of(lo_b), code_of(hi_b)
    n_unknown = 32 - lax.clz(u_lo ^ u_hi)
    low_mask = jnp.where(n_unknown >= 32, -1, lax.shift_left(jnp.int32(1), jnp.minimum(n_unknown, 31)) - 1)
    u0 = u_hi & ~low_mask
    total = n_chunks * tk

    def count_ge(cf):
        return total - count(lambda s, kp: s < cf)

    cnt = count_ge(float_of(u0))
    n_bits = jnp.max(n_unknown)

    def bit_body(b, carry):
        u, cnt = carry
        cand = u | lax.shift_left(jnp.int32(1), n_bits - 1 - b)
        cc = count_ge(float_of(cand))
        ok = cc >= n_top
        return jnp.where(ok, cand, u), jnp.where(ok, cc, cnt)

    u, cnt = lax.fori_loop(0, n_bits, bit_body, (u0, cnt))
    thr = jnp.maximum(float_of(u), f32_lowest)

    tie = jnp.logical_and(cnt > n_top, float_of(u) >= f32_lowest)
    any_tie = jnp.max(jnp.where(tie, 1, 0)) > 0

    @pl.when(any_tie)
    def _():
        need = n_top - count(lambda s, kp: s > thr)

        def pos_body(b, pos):
            cand = pos | lax.shift_left(jnp.int32(1), pos_bits - 1 - b)
            below = count(lambda s, kp: jnp.logical_and(s == thr, kp < cand))
            return jnp.where(below < need, cand, pos)

        pos = lax.fori_loop(0, pos_bits, pos_body, jnp.zeros((1, tq), I32))
        cut = jnp.where(tie, pos, INT_MAX)

        def drop_body(c, carry):
            off = pl.multiple_of(c * tk, tk)
            sc = sc_ref[pl.ds(off, tk), :]
            drop = jnp.logical_and(sc == thr, c * tk + kr > cut)
            sc_ref[pl.ds(off, tk), :] = jnp.where(drop, -jnp.inf, sc)
            return carry

        lax.fori_loop(0, n_chunks, drop_body, 0)

    m_ref[...] = jnp.full(m_ref.shape, NEG_INF, F32)
    l_ref[...] = jnp.zeros(l_ref.shape, F32)
    acc_ref[...] = jnp.zeros(acc_ref.shape, F32)
    n_units = DSA_HEADS // 2
    q2 = [jnp.concatenate([qb_ref[0, :, h * HEAD_DIM:(h + 1) * HEAD_DIM] for h in (2 * u, 2 * u + 1)], axis=0)
          for u in range(n_units)]

    def kv_cols(u):
        g = (2 * u) // DSA_GROUP
        return slice(g * HEAD_DIM, (g + 1) * HEAD_DIM)

    def stage_a(c, s_ref):
        off = pl.multiple_of(c * tk, tk)
        for u in range(n_units):
            s_ref[u] = lax.dot_general(kb_ref[0, pl.ds(off, tk), kv_cols(u)], q2[u], _NT,
                                       preferred_element_type=F32)

    def stage_b(c, s_ref):
        off = pl.multiple_of(c * tk, tk)
        bias = jnp.where(sc_ref[pl.ds(off, tk), :] >= thr, 0.0, -jnp.inf)
        bias2 = jnp.concatenate([bias, bias], axis=1)
        for u in range(n_units):
            s = s_ref[u] + bias2
            m_old = m_ref[u]
            m_new = jnp.maximum(m_old, jnp.max(s, axis=0, keepdims=True))
            alpha = jnp.exp2(m_old - m_new)
            p = jnp.exp2(s - m_new)
            l_ref[u] = alpha * l_ref[u] + jnp.sum(p, axis=0, keepdims=True)
            acc_ref[u] = alpha * acc_ref[u] + lax.dot_general(
                vb_ref[0, pl.ds(off, tk), kv_cols(u)], p.astype(BF16), _TN, preferred_element_type=F32)
            m_ref[u] = m_new

    stage_a(0, s0_ref)

    def pair_body(t, carry):
        c = 2 * t
        stage_a(c + 1, s1_ref)
        stage_b(c, s0_ref)
        stage_a(jnp.minimum(c + 2, n_chunks - 1), s0_ref)
        stage_b(c + 1, s1_ref)
        return carry

    lax.fori_loop(0, n_chunks // 2, pair_body, 0)

    @pl.when(n_chunks % 2 == 1)
    def _():
        stage_b(n_chunks - 1, s0_ref)

    for u in range(n_units):
        o_t = acc_ref[u] / l_ref[u]
        for j in range(2):
            h = 2 * u + j
            o_ref[0, :, h * HEAD_DIM:(h + 1) * HEAD_DIM] = o_t[:, j * tq:(j + 1) * tq].T


def _dsa(qb, kb, vb, qi, klo, khi, wi):
    b, s, _ = qb.shape
    tq, tk = _DSA_TQ, _DSA_TK
    assert s % tk == 0
    n_top = min(DSA_TOPK_MAX, s // 4)
    assert tk % n_top == 0
    qrow = lambda width: pl.BlockSpec((1, tq, width), lambda bi, i: (bi, i, 0))
    full = lambda width: _resident((1, s, width), lambda bi, i: (bi, 0, 0))
    return pl.pallas_call(
        functools.partial(_dsa_kernel, n_top=n_top, pos_bits=max(1, (s - 1).bit_length())),
        grid=(b, s // tq),
        in_specs=[qrow(DSA_HEADS * HEAD_DIM), full(DSA_KV_HEADS * HEAD_DIM), full(DSA_KV_HEADS * HEAD_DIM),
                  qrow(IDX_HEADS * IDX_DIM), full(LANES), full(LANES), qrow(LANES)],
        out_specs=qrow(DSA_HEADS * HEAD_DIM),
        out_shape=jax.ShapeDtypeStruct((b, s, DSA_HEADS * HEAD_DIM), F32),
        scratch_shapes=[pltpu.VMEM((s, tq), F32),
                        pltpu.VMEM((n_top, tq), F32),
                        pltpu.VMEM((DSA_HEADS // 2, 1, 2 * tq), F32),
                        pltpu.VMEM((DSA_HEADS // 2, 1, 2 * tq), F32),
                        pltpu.VMEM((DSA_HEADS // 2, HEAD_DIM, 2 * tq), F32),
                        pltpu.VMEM((DSA_HEADS // 2, tk, 2 * tq), F32), pltpu.VMEM((DSA_HEADS // 2, tk, 2 * tq), F32)],
        compiler_params=_params(2),
        name="dsa_attn",
    )(qb, kb, vb, qi, klo, khi, wi)


def _rms(z, g):
    ms = jnp.mean(z * z, axis=-1, keepdims=True)
    return z * lax.rsqrt(ms + NORM_EPS) * g


def _outproj_kernel(ya_ref, yb_ref, ga_ref, gb_ref, x_ref, gt_ref, g_ref, w_ref, o_ref):
    mixed = ga_ref[0] * ya_ref[0] + gb_ref[0] * yb_ref[0]
    z = jnp.dot(mixed.astype(BF16), w_ref[...], preferred_element_type=F32)
    o_ref[0] = x_ref[0] + gt_ref[0] * _rms(z, g_ref[...])


def _outproj(ya, yb, ga, gb, x, gt, g, w, tm):
    b, s, d = x.shape
    row = pl.BlockSpec((1, tm, d), lambda bi, i: (bi, i, 0))
    return pl.pallas_call(
        _outproj_kernel,
        grid=(b, s // tm),
        in_specs=[row, row, row, row, row, pl.BlockSpec((1, 1, d), lambda bi, i: (bi, 0, 0)),
                  pl.BlockSpec((1, d), lambda bi, i: (0, 0)), _resident((d, d), lambda bi, i: (0, 0))],
        out_specs=row,
        out_shape=jax.ShapeDtypeStruct((b, s, d), F32),
        compiler_params=_params(2),
        name="out_proj",
    )(ya, yb, ga, gb, x, gt, g.reshape(1, d), w)


_FF_TILE = 256


def _ffn_kernel(x_ref, sc_ref, sh_ref, gt_ref, gpre_ref, gpost_ref, wg_ref, wu_ref, wd_ref, o_ref, a_ref):
    x = x_ref[0]
    h = (_rms(x, gpre_ref[...]) * (1.0 + sc_ref[0]) + sh_ref[0]).astype(BF16)
    for c0 in range(0, D_FF, _FF_TILE):
        gte = jnp.dot(h, wg_ref[:, c0:c0 + _FF_TILE], preferred_element_type=F32)
        up = jnp.dot(h, wu_ref[:, c0:c0 + _FF_TILE], preferred_element_type=F32)
        a_ref[:, c0:c0 + _FF_TILE] = (gte * jax.nn.sigmoid(gte) * up).astype(BF16)
    f = jnp.dot(a_ref[...], wd_ref[...], preferred_element_type=F32)
    o_ref[0] = x + gt_ref[0] * _rms(f, gpost_ref[...])


def _ffn(x, sc, sh, gt, gpre, gpost, wg, wu, wd, tm):
    b, s, d = x.shape
    ff = wg.shape[1]
    row = pl.BlockSpec((1, tm, d), lambda bi, i: (bi, i, 0))
    mod = pl.BlockSpec((1, 1, d), lambda bi, i: (bi, 0, 0))
    gain = pl.BlockSpec((1, d), lambda bi, i: (0, 0))
    return pl.pallas_call(
        _ffn_kernel,
        grid=(b, s // tm),
        in_specs=[row, mod, mod, mod, gain, gain,
                  _resident((d, ff), lambda bi, i: (0, 0)), _resident((d, ff), lambda bi, i: (0, 0)),
                  _resident((ff, d), lambda bi, i: (0, 0))],
        out_specs=row,
        out_shape=jax.ShapeDtypeStruct((b, s, d), F32),
        scratch_shapes=[pltpu.VMEM((tm, ff), BF16)],
        compiler_params=_params(2),
        name="ffn",
    )(x, sc, sh, gt, gpre.reshape(1, d), gpost.reshape(1, d), wg, wu, wd)


def kernel(x, c, positions, w_ada, b_ada, g_pre_mix, g_post_mix, w_in, w_out, g_pre_ffn, g_post_ffn,
           w_gate, w_up, w_down):
    b, s, d = x.shape
    depth = w_ada.shape[0]
    tm = min(s, 256)

    c_pad = jnp.zeros((8, d), F32).at[:b].set(c)
    mod = _ada(c_pad, w_ada, b_ada)[:, :b]
    tabs = _rope_tables(positions)

    for l in range(depth):
        sh1, sc1, gt1, sh2, sc2, gt2 = [mod[l, :, j * d:(j + 1) * d].reshape(b, 1, d) for j in range(N_MOD)]
        pr = _inproj(x, sc1, sh1, g_pre_mix[l], _arrange_w_in(w_in[l]), tabs, tm)
        ya = _moba(pr["qa"], pr["ka"], pr["va"])
        yb = _dsa(pr["qb"], pr["kb"], pr["vb"], pr["qi"], pr["klo"], pr["khi"], pr["wi"])
        x = _outproj(ya, yb, pr["ga"], pr["gb"], x, gt1, g_post_mix[l], w_out[l].astype(BF16), tm)
        x = _ffn(x, sc2, sh2, gt2, g_pre_ffn[l], g_post_ffn[l],
                 w_gate[l].astype(BF16), w_up[l].astype(BF16), w_down[l].astype(BF16), tm)
    return x
```

```python
import functools

import jax
import jax.numpy as jnp
import numpy as np
from jax import lax
from jax.experimental import pallas as pl
from jax.experimental.pallas import tpu as pltpu

F32 = jnp.float32
BF16 = jnp.bfloat16
I32 = jnp.int32

D_MODEL = 1024
HEAD_DIM = 128
ROT_DIM = HEAD_DIM // 4
ROPE_THETA = 500000.0
MOBA_HEADS = 8
MOBA_BLOCK = 256
MOBA_TOPK = 3
DSA_HEADS = 8
DSA_KV_HEADS = 2
DSA_GROUP = DSA_HEADS // DSA_KV_HEADS
DSA_TOPK_MAX = 256
IDX_HEADS = 8
IDX_DIM = 64
IDX_ROT_DIM = IDX_DIM // 4
D_FF = 2816
N_MOD = 6
NORM_EPS = 1e-6
NEG_INF = -1e30

LANES = 128
VMEM_LIMIT = 56 * 1024 * 1024

_OFF_QA, _OFF_KA, _OFF_VA, _OFF_QB = 0, 1024, 2048, 3072
_OFF_KB, _OFF_VB, _OFF_QI, _OFF_KI, _OFF_WI, _OFF_GA, _OFF_GB = 4096, 4352, 4608, 5120, 5184, 5192, 6216

INT_MIN = -2147483648
INT_MAX = 2147483647


def _params(n_axes):
    return pltpu.CompilerParams(dimension_semantics=("arbitrary",) * n_axes,
                                vmem_limit_bytes=VMEM_LIMIT)


def _resident(block_shape, index_map):
    return pl.BlockSpec(block_shape, index_map, pipeline_mode=pl.Buffered(1))


def _ada_kernel(c_ref, w_ref, b_ref, o_ref):
    c = c_ref[...]
    s = c * jax.nn.sigmoid(c)
    o_ref[0] = jnp.dot(s, w_ref[0], preferred_element_type=F32) + b_ref[0]


def _ada(c_pad, w_ada, b_ada):
    depth, d, n = w_ada.shape
    tn = 1536
    rows = c_pad.shape[0]
    return pl.pallas_call(
        _ada_kernel,
        grid=(depth, n // tn),
        in_specs=[pl.BlockSpec((rows, d), lambda l, j: (0, 0)),
                  pl.BlockSpec((1, d, tn), lambda l, j: (l, 0, j)),
                  pl.BlockSpec((1, 1, tn), lambda l, j: (l, 0, j))],
        out_specs=pl.BlockSpec((1, rows, tn), lambda l, j: (l, 0, j)),
        out_shape=jax.ShapeDtypeStruct((depth, rows, n), F32),
        compiler_params=_params(2),
        name="ada_mod",
    )(c_pad, w_ada, b_ada.reshape(depth, 1, n))


def _rope_table_kernel(pos_ref, inv_ref, sgn_ref, cm_ref, sm_ref, ci_ref, si_ref):
    pos = pos_ref[0].astype(F32)
    ang_m = pos * inv_ref[0:1, :]
    ang_i = pos * inv_ref[1:2, :]
    cm_ref[0] = jnp.cos(ang_m)
    sm_ref[0] = jnp.sin(ang_m) * sgn_ref[0:1, :]
    ci_ref[0] = jnp.cos(ang_i)
    si_ref[0] = jnp.sin(ang_i) * sgn_ref[1:2, :]


def _rope_lane_constants():
    lane = np.arange(LANES)
    half_m, half_i = ROT_DIM // 2, IDX_ROT_DIM // 2
    inv_m = ROPE_THETA ** (-jnp.arange(half_m, dtype=F32) * 2.0 / ROT_DIM)
    inv_i = ROPE_THETA ** (-jnp.arange(half_i, dtype=F32) * 2.0 / IDX_ROT_DIM)
    in_m = lane < ROT_DIM
    in_i = (lane % IDX_DIM) < IDX_ROT_DIM
    inv = jnp.stack([jnp.where(in_m, inv_m[lane % half_m], 0.0),
                     jnp.where(in_i, inv_i[lane % half_i], 0.0)]).astype(F32)
    sgn_m = np.where(lane < half_m, -1.0, 1.0)
    sgn_i = np.where((lane % IDX_DIM) < half_i, -1.0, 1.0)
    sgn = jnp.asarray(np.stack([sgn_m, sgn_i]), F32)
    return inv, sgn


def _rope_tables(positions):
    b, s = positions.shape
    tm = min(s, 1024)
    inv, sgn = _rope_lane_constants()
    tab = jax.ShapeDtypeStruct((b, s, LANES), F32)
    tspec = pl.BlockSpec((1, tm, LANES), lambda bi, i: (bi, i, 0))
    cspec = pl.BlockSpec((2, LANES), lambda bi, i: (0, 0))
    return pl.pallas_call(
        _rope_table_kernel,
        grid=(b, s // tm),
        in_specs=[pl.BlockSpec((1, tm, 1), lambda bi, i: (bi, i, 0)), cspec, cspec],
        out_specs=[tspec] * 4,
        out_shape=[tab] * 4,
        compiler_params=_params(2),
        name="rope_tables",
    )(positions.reshape(b, s, 1), inv, sgn)


def _rope(t, cos, sin, period, half):
    lane = lax.broadcasted_iota(I32, t.shape, 1)
    first = (lane % period) < half
    partner = jnp.where(first, pltpu.roll(t, LANES - half, 1), pltpu.roll(t, half, 1))
    return t * cos + partner * sin


_IN_TILE = 256
QK_EXP2_SCALE = (HEAD_DIM ** -0.5) * 1.4426950408889634


def _inproj_kernel(x_ref, sc_ref, sh_ref, g_ref, w_ref, cm_ref, sm_ref, ci_ref, si_ref,
                   qa_ref, ka_ref, va_ref, qb_ref, kb_ref, vb_ref, qi_ref, klo_ref, khi_ref,
                   wi_ref, ga_ref, gb_ref, h_ref):
    x = x_ref[0]
    ms = jnp.mean(x * x, axis=-1, keepdims=True)
    y = x * lax.rsqrt(ms + NORM_EPS) * g_ref[...]
    h_ref[...] = (y * (1.0 + sc_ref[0]) + sh_ref[0]).astype(BF16)

    def proj(c0, width):
        return jnp.dot(h_ref[...], w_ref[:, c0:c0 + width], preferred_element_type=F32)

    def rope_cols(t, kind):
        if kind == "main":
            cos, sin, period, half = cm_ref[0], sm_ref[0], HEAD_DIM, ROT_DIM // 2
        else:
            cos, sin, period, half = ci_ref[0], si_ref[0], IDX_DIM, IDX_ROT_DIM // 2
        slabs = [_rope(t[:, j:j + LANES], cos, sin, period, half) for j in range(0, t.shape[1], LANES)]
        return slabs[0] if len(slabs) == 1 else jnp.concatenate(slabs, axis=1)

    col = 0
    plan = [(qa_ref, 1024, "query"), (ka_ref, 1024, "main"), (va_ref, 1024, None), (qb_ref, 1024, "query"),
            (kb_ref, 256, "main"), (vb_ref, 256, None), (qi_ref, 512, "idx"),
            (klo_ref, 128, "idx"), (khi_ref, 128, "idx"), (wi_ref, 128, "wi"),
            (ga_ref, 1024, "gate"), (gb_ref, 1024, "gate")]
    for ref, width, kind in plan:
        for c0 in range(0, width, _IN_TILE):
            wd = min(_IN_TILE, width - c0)
            t = proj(col + c0, wd)
            if kind in ("main", "idx"):
                t = rope_cols(t, kind)
            elif kind == "query":
                t = rope_cols(t, "main") * QK_EXP2_SCALE
            elif kind == "wi":
                t = t * (IDX_HEADS ** -0.5)
            elif kind == "gate":
                t = jax.nn.sigmoid(t)
            ref[0, :, c0:c0 + wd] = t.astype(ref.dtype)
        col += width


def _arrange_w_in(w_in_l):
    d = w_in_l.shape[0]
    w_in_l = w_in_l.astype(BF16)
    ki = w_in_l[:, _OFF_KI:_OFF_KI + IDX_DIM]
    z64 = jnp.zeros((d, IDX_DIM), w_in_l.dtype)
    wi = w_in_l[:, _OFF_WI:_OFF_WI + IDX_HEADS]
    zwi = jnp.zeros((d, LANES - IDX_HEADS), w_in_l.dtype)
    parts = [w_in_l[:, :_OFF_KI], ki, z64, z64, ki, wi, zwi, w_in_l[:, _OFF_GA:]]
    return jnp.concatenate(parts, axis=1)


def _inproj(x, sc, sh, g, w_arr, tabs, tm):
    b, s, d = x.shape
    n = w_arr.shape[1]
    row = lambda width: pl.BlockSpec((1, tm, width), lambda bi, i: (bi, i, 0))
    mod = pl.BlockSpec((1, 1, d), lambda bi, i: (bi, 0, 0))
    outs = [("qa", 1024, BF16), ("ka", 1024, BF16), ("va", 1024, BF16), ("qb", 1024, BF16),
            ("kb", 256, BF16), ("vb", 256, BF16), ("qi", 512, BF16), ("klo", 128, BF16), ("khi", 128, BF16),
            ("wi", 128, F32), ("ga", 1024, F32), ("gb", 1024, F32)]
    res = pl.pallas_call(
        _inproj_kernel,
        grid=(b, s // tm),
        in_specs=[row(d), mod, mod, pl.BlockSpec((1, d), lambda bi, i: (0, 0)),
                  _resident((d, n), lambda bi, i: (0, 0)),
                  row(LANES), row(LANES), row(LANES), row(LANES)],
        out_specs=[row(wd) for _, wd, _ in outs],
        out_shape=[jax.ShapeDtypeStruct((b, s, wd), dt) for _, wd, dt in outs],
        scratch_shapes=[pltpu.VMEM((tm, d), BF16)],
        compiler_params=_params(2),
        name="in_proj",
    )(x, sc, sh, g.reshape(1, d), w_arr, *tabs)
    return dict(zip([nm for nm, _, _ in outs], res))


_NT = (((1,), (1,)), ((), ()))
_TN = (((0,), (0,)), ((), ()))


def _moba_kernel(q_ref, k_ref, v_ref, o_ref, kmean_ref, bias_ref, m_ref, l_ref, acc_ref, s0_ref, s1_ref,
                 *, nb, nbp):
    i = pl.program_id(1)
    tq = MOBA_BLOCK

    @pl.when(i == 0)
    def _():
        kmean_ref[...] = jnp.zeros_like(kmean_ref)

        def mean_body(j, carry):
            kb = k_ref[0, pl.ds(pl.multiple_of(j * tq, tq), tq), :].astype(F32)
            kmean_ref[pl.ds(j, 1), :] = jnp.sum(kb, axis=0, keepdims=True) * (1.0 / tq)
            return carry

        lax.fori_loop(0, nb, mean_body, 0)

    off_own = pl.multiple_of(i * tq, tq)
    blk = lax.broadcasted_iota(I32, (nbp, tq), 0)
    kr = lax.broadcasted_iota(I32, (tq, tq), 0)
    qc = lax.broadcasted_iota(I32, (tq, tq), 1)
    for h in range(MOBA_HEADS):
        hs = slice(h * HEAD_DIM, (h + 1) * HEAD_DIM)
        q = q_ref[0, :, hs]
        km = kmean_ref[0:nbp, hs].astype(BF16)
        gate = lax.dot_general(km, q, _NT, preferred_element_type=F32)
        gate = jnp.where(blk < i, gate, NEG_INF)
        chosen = jnp.zeros(gate.shape, F32)
        for _ in range(MOBA_TOPK):
            mx = jnp.max(gate, axis=0, keepdims=True)
            first = jnp.min(jnp.where(gate == mx, blk, LANES), axis=0, keepdims=True)
            hit = blk == first
            chosen = jnp.where(hit, 1.0, chosen)
            gate = jnp.where(hit, -jnp.inf, gate)
        bias_ref[h] = jnp.where(blk < i, jnp.where(chosen > 0.0, 0.0, -jnp.inf), -jnp.inf)

        s = lax.dot_general(k_ref[0, pl.ds(off_own, tq), hs], q, _NT, preferred_element_type=F32)
        s = jnp.where(kr <= qc, s, -jnp.inf)
        m0 = jnp.max(s, axis=0, keepdims=True)
        p = jnp.exp2(s - m0)
        m_ref[h] = m0
        l_ref[h] = jnp.sum(p, axis=0, keepdims=True)
        acc_ref[h] = lax.dot_general(v_ref[0, pl.ds(off_own, tq), hs], p.astype(BF16), _TN,
                                     preferred_element_type=F32)

    def stage_a(j, s_ref):
        off = pl.multiple_of(j * tq, tq)
        for h in range(MOBA_HEADS):
            hs = slice(h * HEAD_DIM, (h + 1) * HEAD_DIM)
            s_ref[h] = lax.dot_general(k_ref[0, pl.ds(off, tq), hs], q_ref[0, :, hs], _NT,
                                       preferred_element_type=F32)

    def stage_b(j, s_ref):
        off = pl.multiple_of(j * tq, tq)
        for h in range(MOBA_HEADS):
            hs = slice(h * HEAD_DIM, (h + 1) * HEAD_DIM)
            s = s_ref[h]
            gate_row = bias_ref[h, pl.ds(j, 1), :]
            m_old = m_ref[h]
            m_new = jnp.maximum(m_old, jnp.max(s, axis=0, keepdims=True) + gate_row)
            alpha = jnp.exp2(m_old - m_new)
            p = jnp.exp2(s - (m_new - gate_row))
            l_ref[h] = alpha * l_ref[h] + jnp.sum(p, axis=0, keepdims=True)
            acc_ref[h] = alpha * acc_ref[h] + lax.dot_general(
                v_ref[0, pl.ds(off, tq), hs], p.astype(BF16), _TN, preferred_element_type=F32)
            m_ref[h] = m_new

    @pl.when(i > 0)
    def _():
        stage_a(0, s0_ref)

    def pair_body(t, carry):
        j = 2 * t
        stage_a(j + 1, s1_ref)
        stage_b(j, s0_ref)
        stage_a(j + 2, s0_ref)
        stage_b(j + 1, s1_ref)
        return carry

    lax.fori_loop(0, i // 2, pair_body, 0)

    @pl.when(i % 2 == 1)
    def _():
        stage_b(i - 1, s0_ref)

    for h in range(MOBA_HEADS):
        o_ref[0, :, h * HEAD_DIM:(h + 1) * HEAD_DIM] = (acc_ref[h] / l_ref[h]).T


def _moba(qa, ka, va):
    b, s, w = qa.shape
    tq = MOBA_BLOCK
    nb = s // tq
    nbp = -(-nb // 8) * 8
    assert s % tq == 0 and nbp <= LANES
    return pl.pallas_call(
        functools.partial(_moba_kernel, nb=nb, nbp=nbp),
        grid=(b, nb),
        in_specs=[pl.BlockSpec((1, tq, w), lambda bi, i: (bi, i, 0)),
                  _resident((1, s, w), lambda bi, i: (bi, 0, 0)),
                  _resident((1, s, w), lambda bi, i: (bi, 0, 0))],
        out_specs=pl.BlockSpec((1, tq, w), lambda bi, i: (bi, i, 0)),
        out_shape=jax.ShapeDtypeStruct((b, s, w), F32),
        scratch_shapes=[pltpu.VMEM((LANES, w), F32),
                        pltpu.VMEM((MOBA_HEADS, nbp, tq), F32),
                        pltpu.VMEM((MOBA_HEADS, 1, tq), F32), pltpu.VMEM((MOBA_HEADS, 1, tq), F32),
                        pltpu.VMEM((MOBA_HEADS, HEAD_DIM, tq), F32),
                        pltpu.VMEM((MOBA_HEADS, tq, tq), F32), pltpu.VMEM((MOBA_HEADS, tq, tq), F32)],
        compiler_params=_params(2),
        name="moba_attn",
    )(qa, ka, va)


_DSA_TQ = 128
_DSA_TK = 512
_DSA_SLAB = 64


def _dsa_kernel(qb_ref, kb_ref, vb_ref, qi_ref, klo_ref, khi_ref, wi_ref, o_ref,
                sc_ref, gmax_ref, m_ref, l_ref, acc_ref, s0_ref, s1_ref, p_ref, *, n_top, pos_bits):
    i = pl.program_id(1)
    tq, tk, slab = _DSA_TQ, _DSA_TK, _DSA_SLAB
    n_chunks = (i * tq + tq + tk - 1) // tk
    f32_lowest = float(np.finfo(np.float32).min)

    kr = lax.broadcasted_iota(I32, (tk, tq), 0)
    qpos = i * tq + lax.broadcasted_iota(I32, (tk, tq), 1)

    wt = (wi_ref[0] * (IDX_DIM ** -0.5)).T
    wrow = [wt[h:h + 1, :] for h in range(IDX_HEADS)]
    pairs = IDX_HEADS // 2
    qi2 = [jnp.concatenate([qi_ref[0, :, p * LANES:(p + 1) * LANES] for p in (2 * u, 2 * u + 1)], axis=0)
           for u in range(pairs // 2)]
    gmax_ref[...] = jnp.full(gmax_ref.shape, -jnp.inf, F32)

    def idx_body(c, carry):
        off = pl.multiple_of(c * tk, tk)
        units = [(u, par) for u in range(pairs // 2) for par in (0, 1)]

        def logits(u, par):
            kref = klo_ref if par == 0 else khi_ref
            return lax.dot_general(kref[0, pl.ds(off, tk), :], qi2[u], _NT, preferred_element_type=F32)

        sc = jnp.zeros((tk, tq), F32)
        nxt = logits(*units[0])
        for n, (u, par) in enumerate(units):
            lg = nxt
            if n + 1 < len(units):
                nxt = logits(*units[n + 1])
            for j in range(2):
                h = 2 * (2 * u + j) + par
                sc = sc + wrow[h] * jnp.maximum(lg[:, j * tq:(j + 1) * tq], 0.0)
        sc = jnp.where(c * tk + kr <= qpos, sc, -jnp.inf)
        sc_ref[pl.ds(off, tk), :] = sc
        part = sc[0:n_top]
        for r in range(1, tk // n_top):
            part = jnp.maximum(part, sc[r * n_top:(r + 1) * n_top])
        gmax_ref[...] = jnp.maximum(gmax_ref[...], part)
        return carry

    lax.fori_loop(0, n_chunks, idx_body, 0)

    def code_of(x):
        bits = lax.bitcast_convert_type(x, I32)
        return bits ^ ((bits >> 31) & INT_MAX) ^ INT_MIN

    def float_of(u):
        ci = u ^ INT_MIN
        return lax.bitcast_convert_type(ci ^ ((ci >> 31) & INT_MAX), F32)

    srow = lax.broadcasted_iota(I32, (slab, tq), 0)

    def count(pred):
        def body(c, acc):
            for r in range(tk // slab):
                off = pl.multiple_of(c * tk + r * slab, slab)
                acc = acc + jnp.where(pred(sc_ref[pl.ds(off, slab), :], off + srow), 1, 0)
            return acc
        acc = lax.fori_loop(0, n_chunks, body, jnp.zeros((slab, tq), I32))
        return jnp.sum(acc, axis=0, keepdims=True)

    gm = gmax_ref[...]
    lo_b = jnp.maximum(jnp.min(gm, axis=0, keepdims=True), f32_lowest)
    hi_b = jnp.max(gm, axis=0, keepdims=True)
    u_lo, u_hi = code_of(lo_b), code_of(hi_b)
    n_unknown = 32 - lax.clz(u_lo ^ u_hi)
    low_mask = jnp.where(n_unknown >= 32, -1, lax.shift_left(jnp.int32(1), jnp.minimum(n_unknown, 31)) - 1)
    u0 = u_hi & ~low_mask
    total = n_chunks * tk

    def count_ge(cf):
        return total - count(lambda s, kp: s < cf)

    cnt = count_ge(float_of(u0))
    n_bits = jnp.max(n_unknown)

    def bit_body(b, carry):
        u, cnt = carry
        cand = u | lax.shift_left(jnp.int32(1), n_bits - 1 - b)
        cc = count_ge(float_of(cand))
        ok = cc >= n_top
        return jnp.where(ok, cand, u), jnp.where(ok, cc, cnt)

    u, cnt = lax.fori_loop(0, n_bits, bit_body, (u0, cnt))
    thr = jnp.maximum(float_of(u), f32_lowest)

    tie = jnp.logical_and(cnt > n_top, float_of(u) >= f32_lowest)
    any_tie = jnp.max(jnp.where(tie, 1, 0)) > 0

    @pl.when(any_tie)
    def _():
        need = n_top - count(lambda s, kp: s > thr)

        def pos_body(b, pos):
            cand = pos | lax.shift_left(jnp.int32(1), pos_bits - 1 - b)
            below = count(lambda s, kp: jnp.logical_and(s == thr, kp < cand))
            return jnp.where(below < need, cand, pos)

        pos = lax.fori_loop(0, pos_bits, pos_body, jnp.zeros((1, tq), I32))
        cut = jnp.where(tie, pos, INT_MAX)

        def drop_body(c, carry):
            off = pl.multiple_of(c * tk, tk)
            sc = sc_ref[pl.ds(off, tk), :]
            drop = jnp.logical_and(sc == thr, c * tk + kr > cut)
            sc_ref[pl.ds(off, tk), :] = jnp.where(drop, -jnp.inf, sc)
            return carry

        lax.fori_loop(0, n_chunks, drop_body, 0)

    m_ref[...] = jnp.full(m_ref.shape, NEG_INF, F32)
    l_ref[...] = jnp.zeros(l_ref.shape, F32)
    acc_ref[...] = jnp.zeros(acc_ref.shape, F32)
    n_units = DSA_HEADS // 2
    q2 = [jnp.concatenate([qb_ref[0, :, h * HEAD_DIM:(h + 1) * HEAD_DIM] for h in (2 * u, 2 * u + 1)], axis=0)
          for u in range(n_units)]

    def kv_cols(u):
        g = (2 * u) // DSA_GROUP
        return slice(g * HEAD_DIM, (g + 1) * HEAD_DIM)

    all_units = tuple(range(n_units))

    def selection_bias(c):
        off = pl.multiple_of(c * tk, tk)
        bias = jnp.where(sc_ref[pl.ds(off, tk), :] >= thr, 0.0, -jnp.inf)
        return jnp.concatenate([bias, bias], axis=1)

    def stage_a(c, s_ref, bias2, units=all_units):
        off = pl.multiple_of(c * tk, tk)
        for u in units:
            s_ref[u] = bias2 + lax.dot_general(kb_ref[0, pl.ds(off, tk), kv_cols(u)], q2[u], _NT,
                                               preferred_element_type=F32)

    def stage_b(c, s_ref, units=all_units):
        off = pl.multiple_of(c * tk, tk)
        for u in units:
            mx = s_ref[u, 0:slab, :]
            for r in range(1, tk // slab):
                mx = jnp.maximum(mx, s_ref[u, r * slab:(r + 1) * slab, :])
            m_old = m_ref[u]
            m_new = jnp.maximum(m_old, jnp.max(mx, axis=0, keepdims=True))
            alpha = jnp.exp2(m_old - m_new)
            part = jnp.zeros((slab, 2 * tq), F32)
            for r in range(tk // slab):
                p = jnp.exp2(s_ref[u, r * slab:(r + 1) * slab, :] - m_new)
                part = part + p
                p_ref[u, r * slab:(r + 1) * slab, :] = p.astype(BF16)
            l_ref[u] = alpha * l_ref[u] + jnp.sum(part, axis=0, keepdims=True)
            acc_ref[u] = alpha * acc_ref[u] + lax.dot_general(
                vb_ref[0, pl.ds(off, tk), kv_cols(u)], p_ref[u], _TN, preferred_element_type=F32)
            m_ref[u] = m_new

    def stage_ab(ca, sa_ref, cb, sb_ref):
        bias2 = selection_bias(ca)
        for u in all_units:
            stage_a(ca, sa_ref, bias2, units=(u,))
            stage_b(cb, sb_ref, units=(u,))

    stage_a(0, s0_ref, selection_bias(0))

    def pair_body(t, carry):
        c = 2 * t
        stage_ab(c + 1, s1_ref, c, s0_ref)
        stage_ab(jnp.minimum(c + 2, n_chunks - 1), s0_ref, c + 1, s1_ref)
        return carry

    lax.fori_loop(0, n_chunks // 2, pair_body, 0)

    @pl.when(n_chunks % 2 == 1)
    def _():
        stage_b(n_chunks - 1, s0_ref)

    for u in range(n_units):
        o_t = acc_ref[u] / l_ref[u]
        for j in range(2):
            h = 2 * u + j
            o_ref[0, :, h * HEAD_DIM:(h + 1) * HEAD_DIM] = o_t[:, j * tq:(j + 1) * tq].T


def _dsa(qb, kb, vb, qi, klo, khi, wi):
    b, s, _ = qb.shape
    tq, tk = _DSA_TQ, _DSA_TK
    assert s % tk == 0
    n_top = min(DSA_TOPK_MAX, s // 4)
    assert tk % n_top == 0
    qrow = lambda width: pl.BlockSpec((1, tq, width), lambda bi, i: (bi, i, 0))
    full = lambda width: _resident((1, s, width), lambda bi, i: (bi, 0, 0))
    return pl.pallas_call(
        functools.partial(_dsa_kernel, n_top=n_top, pos_bits=max(1, (s - 1).bit_length())),
        grid=(b, s // tq),
        in_specs=[qrow(DSA_HEADS * HEAD_DIM), full(DSA_KV_HEADS * HEAD_DIM), full(DSA_KV_HEADS * HEAD_DIM),
                  qrow(IDX_HEADS * IDX_DIM), full(LANES), full(LANES), qrow(LANES)],
        out_specs=qrow(DSA_HEADS * HEAD_DIM),
        out_shape=jax.ShapeDtypeStruct((b, s, DSA_HEADS * HEAD_DIM), F32),
        scratch_shapes=[pltpu.VMEM((s, tq), F32),
                        pltpu.VMEM((n_top, tq), F32),
                        pltpu.VMEM((DSA_HEADS // 2, 1, 2 * tq), F32),
                        pltpu.VMEM((DSA_HEADS // 2, 1, 2 * tq), F32),
                        pltpu.VMEM((DSA_HEADS // 2, HEAD_DIM, 2 * tq), F32),
                        pltpu.VMEM((DSA_HEADS // 2, tk, 2 * tq), F32), pltpu.VMEM((DSA_HEADS // 2, tk, 2 * tq), F32),
                        pltpu.VMEM((DSA_HEADS // 2, tk, 2 * tq), BF16)],
        compiler_params=_params(2),
        name="dsa_attn",
    )(qb, kb, vb, qi, klo, khi, wi)


def _rms(z, g):
    ms = jnp.mean(z * z, axis=-1, keepdims=True)
    return z * lax.rsqrt(ms + NORM_EPS) * g


def _outproj_kernel(ya_ref, yb_ref, ga_ref, gb_ref, x_ref, gt_ref, g_ref, w_ref, o_ref):
    mixed = ga_ref[0] * ya_ref[0] + gb_ref[0] * yb_ref[0]
    z = jnp.dot(mixed.astype(BF16), w_ref[...], preferred_element_type=F32)
    o_ref[0] = x_ref[0] + gt_ref[0] * _rms(z, g_ref[...])


def _outproj(ya, yb, ga, gb, x, gt, g, w, tm):
    b, s, d = x.shape
    row = pl.BlockSpec((1, tm, d), lambda bi, i: (bi, i, 0))
    return pl.pallas_call(
        _outproj_kernel,
        grid=(b, s // tm),
        in_specs=[row, row, row, row, row, pl.BlockSpec((1, 1, d), lambda bi, i: (bi, 0, 0)),
                  pl.BlockSpec((1, d), lambda bi, i: (0, 0)), _resident((d, d), lambda bi, i: (0, 0))],
        out_specs=row,
        out_shape=jax.ShapeDtypeStruct((b, s, d), F32),
        compiler_params=_params(2),
        name="out_proj",
    )(ya, yb, ga, gb, x, gt, g.reshape(1, d), w)


_FF_TILE = 256


def _ffn_kernel(x_ref, sc_ref, sh_ref, gt_ref, gpre_ref, gpost_ref, wg_ref, wu_ref, wd_ref, o_ref, a_ref):
    x = x_ref[0]
    h = (_rms(x, gpre_ref[...]) * (1.0 + sc_ref[0]) + sh_ref[0]).astype(BF16)
    for c0 in range(0, D_FF, _FF_TILE):
        gte = jnp.dot(h, wg_ref[:, c0:c0 + _FF_TILE], preferred_element_type=F32)
        up = jnp.dot(h, wu_ref[:, c0:c0 + _FF_TILE], preferred_element_type=F32)
        a_ref[:, c0:c0 + _FF_TILE] = (gte * jax.nn.sigmoid(gte) * up).astype(BF16)
    f = jnp.dot(a_ref[...], wd_ref[...], preferred_element_type=F32)
    o_ref[0] = x + gt_ref[0] * _rms(f, gpost_ref[...])


def _ffn(x, sc, sh, gt, gpre, gpost, wg, wu, wd, tm):
    b, s, d = x.shape
    ff = wg.shape[1]
    row = pl.BlockSpec((1, tm, d), lambda bi, i: (bi, i, 0))
    mod = pl.BlockSpec((1, 1, d), lambda bi, i: (bi, 0, 0))
    gain = pl.BlockSpec((1, d), lambda bi, i: (0, 0))
    return pl.pallas_call(
        _ffn_kernel,
        grid=(b, s // tm),
        in_specs=[row, mod, mod, mod, gain, gain,
                  _resident((d, ff), lambda bi, i: (0, 0)), _resident((d, ff), lambda bi, i: (0, 0)),
                  _resident((ff, d), lambda bi, i: (0, 0))],
        out_specs=row,
        out_shape=jax.ShapeDtypeStruct((b, s, d), F32),
        scratch_shapes=[pltpu.VMEM((tm, ff), BF16)],
        compiler_params=_params(2),
        name="ffn",
    )(x, sc, sh, gt, gpre.reshape(1, d), gpost.reshape(1, d), wg, wu, wd)


def kernel(x, c, positions, w_ada, b_ada, g_pre_mix, g_post_mix, w_in, w_out, g_pre_ffn, g_post_ffn,
           w_gate, w_up, w_down):
    b, s, d = x.shape
    depth = w_ada.shape[0]
    tm = min(s, 256)

    c_pad = jnp.zeros((8, d), F32).at[:b].set(c)
    mod = _ada(c_pad, w_ada, b_ada)[:, :b]
    tabs = _rope_tables(positions)

    for l in range(depth):
        sh1, sc1, gt1, sh2, sc2, gt2 = [mod[l, :, j * d:(j + 1) * d].reshape(b, 1, d) for j in range(N_MOD)]
        pr = _inproj(x, sc1, sh1, g_pre_mix[l], _arrange_w_in(w_in[l]), tabs, tm)
        ya = _moba(pr["qa"], pr["ka"], pr["va"])
        yb = _dsa(pr["qb"], pr["kb"], pr["vb"], pr["qi"], pr["klo"], pr["khi"], pr["wi"])
        x = _outproj(ya, yb, pr["ga"], pr["gb"], x, gt1, g_post_mix[l], w_out[l].astype(BF16), tm)
        x = _ffn(x, sc2, sh2, gt2, g_pre_ffn[l], g_post_ffn[l],
                 w_gate[l].astype(BF16), w_up[l].astype(BF16), w_down[l].astype(BF16), tm)
    return x
```

```python
import functools

import jax
import jax.numpy as jnp
import numpy as np
from jax import lax
from jax.experimental import pallas as pl
from jax.experimental.pallas import tpu as pltpu

F32 = jnp.float32
BF16 = jnp.bfloat16
I32 = jnp.int32

D_MODEL = 1024
HEAD_DIM = 128
ROT_DIM = HEAD_DIM // 4
ROPE_THETA = 500000.0
MOBA_HEADS = 8
MOBA_BLOCK = 256
MOBA_TOPK = 3
DSA_HEADS = 8
DSA_KV_HEADS = 2
DSA_GROUP = DSA_HEADS // DSA_KV_HEADS
DSA_TOPK_MAX = 256
IDX_HEADS = 8
IDX_DIM = 64
IDX_ROT_DIM = IDX_DIM // 4
D_FF = 2816
N_MOD = 6
NORM_EPS = 1e-6
NEG_INF = -1e30

LANES = 128
VMEM_LIMIT = 56 * 1024 * 1024

_OFF_QA, _OFF_KA, _OFF_VA, _OFF_QB = 0, 1024, 2048, 3072
_OFF_KB, _OFF_VB, _OFF_QI, _OFF_KI, _OFF_WI, _OFF_GA, _OFF_GB = 4096, 4352, 4608, 5120, 5184, 5192, 6216

INT_MIN = -2147483648
INT_MAX = 2147483647


def _params(n_axes):
    return pltpu.CompilerParams(dimension_semantics=("arbitrary",) * n_axes,
                                vmem_limit_bytes=VMEM_LIMIT)


def _resident(block_shape, index_map):
    return pl.BlockSpec(block_shape, index_map, pipeline_mode=pl.Buffered(1))


def _ada_kernel(c_ref, w_ref, b_ref, o_ref):
    c = c_ref[...]
    s = c * jax.nn.sigmoid(c)
    o_ref[0] = jnp.dot(s, w_ref[0], preferred_element_type=F32) + b_ref[0]


def _ada(c_pad, w_ada, b_ada):
    depth, d, n = w_ada.shape
    tn = 1536
    rows = c_pad.shape[0]
    return pl.pallas_call(
        _ada_kernel,
        grid=(depth, n // tn),
        in_specs=[pl.BlockSpec((rows, d), lambda l, j: (0, 0)),
                  pl.BlockSpec((1, d, tn), lambda l, j: (l, 0, j)),
                  pl.BlockSpec((1, 1, tn), lambda l, j: (l, 0, j))],
        out_specs=pl.BlockSpec((1, rows, tn), lambda l, j: (l, 0, j)),
        out_shape=jax.ShapeDtypeStruct((depth, rows, n), F32),
        compiler_params=_params(2),
        name="ada_mod",
    )(c_pad, w_ada, b_ada.reshape(depth, 1, n))


def _rope_table_kernel(pos_ref, inv_ref, sgn_ref, cm_ref, sm_ref, ci_ref, si_ref):
    pos = pos_ref[0].astype(F32)
    ang_m = pos * inv_ref[0:1, :]
    ang_i = pos * inv_ref[1:2, :]
    cm_ref[0] = jnp.cos(ang_m)
    sm_ref[0] = jnp.sin(ang_m) * sgn_ref[0:1, :]
    ci_ref[0] = jnp.cos(ang_i)
    si_ref[0] = jnp.sin(ang_i) * sgn_ref[1:2, :]


def _rope_lane_constants():
    lane = np.arange(LANES)
    half_m, half_i = ROT_DIM // 2, IDX_ROT_DIM // 2
    inv_m = ROPE_THETA ** (-jnp.arange(half_m, dtype=F32) * 2.0 / ROT_DIM)
    inv_i = ROPE_THETA ** (-jnp.arange(half_i, dtype=F32) * 2.0 / IDX_ROT_DIM)
    in_m = lane < ROT_DIM
    in_i = (lane % IDX_DIM) < IDX_ROT_DIM
    inv = jnp.stack([jnp.where(in_m, inv_m[lane % half_m], 0.0),
                     jnp.where(in_i, inv_i[lane % half_i], 0.0)]).astype(F32)
    sgn_m = np.where(lane < half_m, -1.0, 1.0)
    sgn_i = np.where((lane % IDX_DIM) < half_i, -1.0, 1.0)
    sgn = jnp.asarray(np.stack([sgn_m, sgn_i]), F32)
    return inv, sgn


def _rope_tables(positions):
    b, s = positions.shape
    tm = min(s, 1024)
    inv, sgn = _rope_lane_constants()
    tab = jax.ShapeDtypeStruct((b, s, LANES), F32)
    tspec = pl.BlockSpec((1, tm, LANES), lambda bi, i: (bi, i, 0))
    cspec = pl.BlockSpec((2, LANES), lambda bi, i: (0, 0))
    return pl.pallas_call(
        _rope_table_kernel,
        grid=(b, s // tm),
        in_specs=[pl.BlockSpec((1, tm, 1), lambda bi, i: (bi, i, 0)), cspec, cspec],
        out_specs=[tspec] * 4,
        out_shape=[tab] * 4,
        compiler_params=_params(2),
        name="rope_tables",
    )(positions.reshape(b, s, 1), inv, sgn)


def _rope(t, cos, sin, period, half):
    lane = lax.broadcasted_iota(I32, t.shape, 1)
    first = (lane % period) < half
    partner = jnp.where(first, pltpu.roll(t, LANES - half, 1), pltpu.roll(t, half, 1))
    return t * cos + partner * sin


_IN_TILE = 256
QK_EXP2_SCALE = (HEAD_DIM ** -0.5) * 1.4426950408889634


def _inproj_kernel(x_ref, sc_ref, sh_ref, g_ref, w_ref, cm_ref, sm_ref, ci_ref, si_ref,
                   qa_ref, ka_ref, va_ref, qb_ref, kb_ref, vb_ref, qi_ref, klo_ref, khi_ref,
                   wi_ref, ga_ref, gb_ref, h_ref):
    x = x_ref[0]
    ms = jnp.mean(x * x, axis=-1, keepdims=True)
    y = x * lax.rsqrt(ms + NORM_EPS) * g_ref[...]
    h_ref[...] = (y * (1.0 + sc_ref[0]) + sh_ref[0]).astype(BF16)

    def proj(c0, width):
        return jnp.dot(h_ref[...], w_ref[:, c0:c0 + width], preferred_element_type=F32)

    def rope_cols(t, kind):
        if kind == "main":
            cos, sin, period, half = cm_ref[0], sm_ref[0], HEAD_DIM, ROT_DIM // 2
        else:
            cos, sin, period, half = ci_ref[0], si_ref[0], IDX_DIM, IDX_ROT_DIM // 2
        slabs = [_rope(t[:, j:j + LANES], cos, sin, period, half) for j in range(0, t.shape[1], LANES)]
        return slabs[0] if len(slabs) == 1 else jnp.concatenate(slabs, axis=1)

    col = 0
    plan = [(qa_ref, 1024, "query"), (ka_ref, 1024, "main"), (va_ref, 1024, None), (qb_ref, 1024, "query"),
            (kb_ref, 256, "main"), (vb_ref, 256, None), (qi_ref, 512, "idx"),
            (klo_ref, 128, "idx"), (khi_ref, 128, "idx"), (wi_ref, 128, "wi"),
            (ga_ref, 1024, "gate"), (gb_ref, 1024, "gate")]
    for ref, width, kind in plan:
        for c0 in range(0, width, _IN_TILE):
            wd = min(_IN_TILE, width - c0)
            t = proj(col + c0, wd)
            if kind in ("main", "idx"):
                t = rope_cols(t, kind)
            elif kind == "query":
                t = rope_cols(t, "main") * QK_EXP2_SCALE
            elif kind == "wi":
                t = t * (IDX_HEADS ** -0.5)
            elif kind == "gate":
                t = jax.nn.sigmoid(t)
            ref[0, :, c0:c0 + wd] = t.astype(ref.dtype)
        col += width


def _arrange_w_in(w_in_l):
    d = w_in_l.shape[0]
    w_in_l = w_in_l.astype(BF16)
    ki = w_in_l[:, _OFF_KI:_OFF_KI + IDX_DIM]
    z64 = jnp.zeros((d, IDX_DIM), w_in_l.dtype)
    wi = w_in_l[:, _OFF_WI:_OFF_WI + IDX_HEADS]
    zwi = jnp.zeros((d, LANES - IDX_HEADS), w_in_l.dtype)
    parts = [w_in_l[:, :_OFF_KI], ki, z64, z64, ki, wi, zwi, w_in_l[:, _OFF_GA:]]
    return jnp.concatenate(parts, axis=1)


def _inproj(x, sc, sh, g, w_arr, tabs, tm):
    b, s, d = x.shape
    n = w_arr.shape[1]
    row = lambda width: pl.BlockSpec((1, tm, width), lambda bi, i: (bi, i, 0))
    mod = pl.BlockSpec((1, 1, d), lambda bi, i: (bi, 0, 0))
    outs = [("qa", 1024, BF16), ("ka", 1024, BF16), ("va", 1024, BF16), ("qb", 1024, BF16),
            ("kb", 256, BF16), ("vb", 256, BF16), ("qi", 512, BF16), ("klo", 128, BF16), ("khi", 128, BF16),
            ("wi", 128, F32), ("ga", 1024, F32), ("gb", 1024, F32)]
    res = pl.pallas_call(
        _inproj_kernel,
        grid=(b, s // tm),
        in_specs=[row(d), mod, mod, pl.BlockSpec((1, d), lambda bi, i: (0, 0)),
                  _resident((d, n), lambda bi, i: (0, 0)),
                  row(LANES), row(LANES), row(LANES), row(LANES)],
        out_specs=[row(wd) for _, wd, _ in outs],
        out_shape=[jax.ShapeDtypeStruct((b, s, wd), dt) for _, wd, dt in outs],
        scratch_shapes=[pltpu.VMEM((tm, d), BF16)],
        compiler_params=_params(2),
        name="in_proj",
    )(x, sc, sh, g.reshape(1, d), w_arr, *tabs)
    return dict(zip([nm for nm, _, _ in outs], res))


_NT = (((1,), (1,)), ((), ()))
_TN = (((0,), (0,)), ((), ()))


def _moba_kernel(q_ref, k_ref, v_ref, o_ref, kmean_ref, bias_ref, m_ref, l_ref, acc_ref, s0_ref, s1_ref,
                 *, nb, nbp):
    i = pl.program_id(1)
    tq = MOBA_BLOCK

    @pl.when(i == 0)
    def _():
        kmean_ref[...] = jnp.zeros_like(kmean_ref)

        def mean_body(j, carry):
            kb = k_ref[0, pl.ds(pl.multiple_of(j * tq, tq), tq), :].astype(F32)
            kmean_ref[pl.ds(j, 1), :] = jnp.sum(kb, axis=0, keepdims=True) * (1.0 / tq)
            return carry

        lax.fori_loop(0, nb, mean_body, 0)

    off_own = pl.multiple_of(i * tq, tq)
    blk = lax.broadcasted_iota(I32, (nbp, tq), 0)
    kr = lax.broadcasted_iota(I32, (tq, tq), 0)
    qc = lax.broadcasted_iota(I32, (tq, tq), 1)
    for h in range(MOBA_HEADS):
        hs = slice(h * HEAD_DIM, (h + 1) * HEAD_DIM)
        q = q_ref[0, :, hs]
        km = kmean_ref[0:nbp, hs].astype(BF16)
        gate = lax.dot_general(km, q, _NT, preferred_element_type=F32)
        gate = jnp.where(blk < i, gate, NEG_INF)
        chosen = jnp.zeros(gate.shape, F32)
        for _ in range(MOBA_TOPK):
            mx = jnp.max(gate, axis=0, keepdims=True)
            first = jnp.min(jnp.where(gate == mx, blk, LANES), axis=0, keepdims=True)
            hit = blk == first
            chosen = jnp.where(hit, 1.0, chosen)
            gate = jnp.where(hit, -jnp.inf, gate)
        bias_ref[h] = jnp.where(blk < i, jnp.where(chosen > 0.0, 0.0, -jnp.inf), -jnp.inf)

        s = lax.dot_general(k_ref[0, pl.ds(off_own, tq), hs], q, _NT, preferred_element_type=F32)
        s = jnp.where(kr <= qc, s, -jnp.inf)
        m0 = jnp.max(s, axis=0, keepdims=True)
        p = jnp.exp2(s - m0)
        m_ref[h] = m0
        l_ref[h] = jnp.sum(p, axis=0, keepdims=True)
        acc_ref[h] = lax.dot_general(v_ref[0, pl.ds(off_own, tq), hs], p.astype(BF16), _TN,
                                     preferred_element_type=F32)

    def stage_a(j, s_ref):
        off = pl.multiple_of(j * tq, tq)
        for h in range(MOBA_HEADS):
            hs = slice(h * HEAD_DIM, (h + 1) * HEAD_DIM)
            s_ref[h] = lax.dot_general(k_ref[0, pl.ds(off, tq), hs], q_ref[0, :, hs], _NT,
                                       preferred_element_type=F32)

    def stage_b(j, s_ref):
        off = pl.multiple_of(j * tq, tq)
        for h in range(MOBA_HEADS):
            hs = slice(h * HEAD_DIM, (h + 1) * HEAD_DIM)
            s = s_ref[h]
            gate_row = bias_ref[h, pl.ds(j, 1), :]
            m_old = m_ref[h]
            m_new = jnp.maximum(m_old, jnp.max(s, axis=0, keepdims=True) + gate_row)
            alpha = jnp.exp2(m_old - m_new)
            p = jnp.exp2(s - (m_new - gate_row))
            l_ref[h] = alpha * l_ref[h] + jnp.sum(p, axis=0, keepdims=True)
            acc_ref[h] = alpha * acc_ref[h] + lax.dot_general(
                v_ref[0, pl.ds(off, tq), hs], p.astype(BF16), _TN, preferred_element_type=F32)
            m_ref[h] = m_new

    stage_a(0, s0_ref)

    def pair_body(t, carry):
        j = 2 * t
        stage_a(j + 1, s1_ref)
        stage_b(j, s0_ref)
        stage_a(j + 2, s0_ref)
        stage_b(j + 1, s1_ref)
        return carry

    lax.fori_loop(0, i // 2, pair_body, 0)

    @pl.when(i % 2 == 1)
    def _():
        stage_b(i - 1, s0_ref)

    for h in range(MOBA_HEADS):
        o_ref[0, :, h * HEAD_DIM:(h + 1) * HEAD_DIM] = (acc_ref[h] / l_ref[h]).T


def _moba(qa, ka, va):
    b, s, w = qa.shape
    tq = MOBA_BLOCK
    nb = s // tq
    nbp = -(-nb // 8) * 8
    assert s % tq == 0 and nbp <= LANES
    return pl.pallas_call(
        functools.partial(_moba_kernel, nb=nb, nbp=nbp),
        grid=(b, nb),
        in_specs=[pl.BlockSpec((1, tq, w), lambda bi, i: (bi, i, 0)),
                  _resident((1, s, w), lambda bi, i: (bi, 0, 0)),
                  _resident((1, s, w), lambda bi, i: (bi, 0, 0))],
        out_specs=pl.BlockSpec((1, tq, w), lambda bi, i: (bi, i, 0)),
        out_shape=jax.ShapeDtypeStruct((b, s, w), F32),
        scratch_shapes=[pltpu.VMEM((LANES, w), F32),
                        pltpu.VMEM((MOBA_HEADS, nbp, tq), F32),
                        pltpu.VMEM((MOBA_HEADS, 1, tq), F32), pltpu.VMEM((MOBA_HEADS, 1, tq), F32),
                        pltpu.VMEM((MOBA_HEADS, HEAD_DIM, tq), F32),
                        pltpu.VMEM((MOBA_HEADS, tq, tq), F32), pltpu.VMEM((MOBA_HEADS, tq, tq), F32)],
        compiler_params=_params(2),
        name="moba_attn",
    )(qa, ka, va)


_DSA_TQ = 128
_DSA_TK = 512
_DSA_SLAB = 64


def _dsa_kernel(qb_ref, kb_ref, vb_ref, qi_ref, klo_ref, khi_ref, wi_ref, o_ref,
                sc_ref, m_ref, l_ref, acc_ref, s0_ref, s1_ref, p_ref, *, n_top):
    i = pl.program_id(1)
    tq, tk, slab = _DSA_TQ, _DSA_TK, _DSA_SLAB
    n_chunks = (i * tq + tq + tk - 1) // tk
    f32_lowest = float(np.finfo(np.float32).min)

    kr = lax.broadcasted_iota(I32, (tk, tq), 0)
    qpos = i * tq + lax.broadcasted_iota(I32, (tk, tq), 1)

    wt = (wi_ref[0] * (IDX_DIM ** -0.5)).T
    wrow = [wt[h:h + 1, :] for h in range(IDX_HEADS)]
    pairs = IDX_HEADS // 2
    qi2 = [jnp.concatenate([qi_ref[0, :, p * LANES:(p + 1) * LANES] for p in (2 * u, 2 * u + 1)], axis=0)
           for u in range(pairs // 2)]

    def idx_body(c, carry):
        off = pl.multiple_of(c * tk, tk)
        units = [(u, par) for u in range(pairs // 2) for par in (0, 1)]

        def logits(u, par):
            kref = klo_ref if par == 0 else khi_ref
            return lax.dot_general(kref[0, pl.ds(off, tk), :], qi2[u], _NT, preferred_element_type=F32)

        sc = jnp.zeros((tk, tq), F32)
        nxt = logits(*units[0])
        for n, (u, par) in enumerate(units):
            lg = nxt
            if n + 1 < len(units):
                nxt = logits(*units[n + 1])
            for j in range(2):
                h = 2 * (2 * u + j) + par
                sc = sc + wrow[h] * jnp.maximum(lg[:, j * tq:(j + 1) * tq], 0.0)
        sc_ref[pl.ds(off, tk), :] = jnp.where(c * tk + kr <= qpos, sc, -jnp.inf)
        return carry

    lax.fori_loop(0, n_chunks, idx_body, 0)

    def float_of(u):
        ci = u ^ INT_MIN
        return lax.bitcast_convert_type(ci ^ ((ci >> 31) & INT_MAX), F32)

    srow = lax.broadcasted_iota(I32, (slab, tq), 0)

    def count(pred):
        def body(c, acc):
            for r in range(tk // slab):
                off = pl.multiple_of(c * tk + r * slab, slab)
                acc = acc + jnp.where(pred(sc_ref[pl.ds(off, slab), :], off + srow), 1, 0)
            return acc
        acc = lax.fori_loop(0, n_chunks, body, jnp.zeros((slab, tq), I32))
        return jnp.sum(acc, axis=0, keepdims=True)

    total = n_chunks * tk

    def count_ge(cf):
        return total - count(lambda s, kp: s < cf)

    def bit_body(b, carry):
        u, cnt = carry
        cand = u | lax.shift_left(jnp.int32(1), 31 - b)
        cc = count_ge(float_of(cand))
        ok = cc >= n_top
        return jnp.where(ok, cand, u), jnp.where(ok, cc, cnt)

    u, cnt = lax.fori_loop(0, 32, bit_body, (jnp.zeros((1, tq), I32), jnp.full((1, tq), total, I32)))
    thr = jnp.maximum(float_of(u), f32_lowest)

    tie = jnp.logical_and(cnt > n_top, float_of(u) >= f32_lowest)
    any_tie = jnp.max(jnp.where(tie, 1, 0)) > 0

    @pl.when(any_tie)
    def _():
        need = n_top - count(lambda s, kp: s > thr)
        keep_below = jnp.where(tie, need, INT_MAX).astype(F32)
        before = (lax.broadcasted_iota(I32, (tk, tk), 1) < lax.broadcasted_iota(I32, (tk, tk), 0))
        lower = jnp.where(before, 1.0, 0.0).astype(BF16)

        def drop_body(c, seen):
            off = pl.multiple_of(c * tk, tk)
            sc = sc_ref[pl.ds(off, tk), :]
            tied = jnp.where(sc == thr, 1.0, 0.0)
            rank = jnp.dot(lower, tied.astype(BF16), preferred_element_type=F32) + seen
            drop = tied * jnp.where(rank >= keep_below, 1.0, 0.0)
            sc_ref[pl.ds(off, tk), :] = jnp.where(drop > 0.0, -jnp.inf, sc)
            return seen + jnp.sum(tied, axis=0, keepdims=True)

        lax.fori_loop(0, n_chunks, drop_body, jnp.zeros((1, tq), F32))

    m_ref[...] = jnp.full(m_ref.shape, NEG_INF, F32)
    l_ref[...] = jnp.zeros(l_ref.shape, F32)
    acc_ref[...] = jnp.zeros(acc_ref.shape, F32)
    n_units = DSA_HEADS // 2
    q2 = [jnp.concatenate([qb_ref[0, :, h * HEAD_DIM:(h + 1) * HEAD_DIM] for h in (2 * u, 2 * u + 1)], axis=0)
          for u in range(n_units)]

    def kv_cols(u):
        g = (2 * u) // DSA_GROUP
        return slice(g * HEAD_DIM, (g + 1) * HEAD_DIM)

    all_units = tuple(range(n_units))

    def selection_bias(c):
        off = pl.multiple_of(c * tk, tk)
        bias = jnp.where(sc_ref[pl.ds(off, tk), :] >= thr, 0.0, -jnp.inf)
        return jnp.concatenate([bias, bias], axis=1)

    def stage_a(c, s_ref, bias2, units=all_units):
        off = pl.multiple_of(c * tk, tk)
        for u in units:
            s_ref[u] = bias2 + lax.dot_general(kb_ref[0, pl.ds(off, tk), kv_cols(u)], q2[u], _NT,
                                               preferred_element_type=F32)

    def stage_b(c, s_ref, units=all_units):
        off = pl.multiple_of(c * tk, tk)
        for u in units:
            mx = s_ref[u, 0:slab, :]
            for r in range(1, tk // slab):
                mx = jnp.maximum(mx, s_ref[u, r * slab:(r + 1) * slab, :])
            m_old = m_ref[u]
            m_new = jnp.maximum(m_old, jnp.max(mx, axis=0, keepdims=True))
            alpha = jnp.exp2(m_old - m_new)
            part = jnp.zeros((slab, 2 * tq), F32)
            for r in range(tk // slab):
                p = jnp.exp2(s_ref[u, r * slab:(r + 1) * slab, :] - m_new)
                part = part + p
                p_ref[u, r * slab:(r + 1) * slab, :] = p.astype(BF16)
            l_ref[u] = alpha * l_ref[u] + jnp.sum(part, axis=0, keepdims=True)
            acc_ref[u] = alpha * acc_ref[u] + lax.dot_general(
                vb_ref[0, pl.ds(off, tk), kv_cols(u)], p_ref[u], _TN, preferred_element_type=F32)
            m_ref[u] = m_new

    def stage_ab(ca, sa_ref, cb, sb_ref):
        bias2 = selection_bias(ca)
        for u in all_units:
            stage_a(ca, sa_ref, bias2, units=(u,))
            stage_b(cb, sb_ref, units=(u,))

    stage_a(0, s0_ref, selection_bias(0))

    def pair_body(t, carry):
        c = 2 * t
        stage_ab(c + 1, s1_ref, c, s0_ref)
        stage_ab(jnp.minimum(c + 2, n_chunks - 1), s0_ref, c + 1, s1_ref)
        return carry

    lax.fori_loop(0, n_chunks // 2, pair_body, 0)

    @pl.when(n_chunks % 2 == 1)
    def _():
        stage_b(n_chunks - 1, s0_ref)

    for u in range(n_units):
        o_t = acc_ref[u] / l_ref[u]
        for j in range(2):
            h = 2 * u + j
            o_ref[0, :, h * HEAD_DIM:(h + 1) * HEAD_DIM] = o_t[:, j * tq:(j + 1) * tq].T


def _dsa(qb, kb, vb, qi, klo, khi, wi):
    b, s, _ = qb.shape
    tq, tk = _DSA_TQ, _DSA_TK
    assert s % tk == 0
    n_top = min(DSA_TOPK_MAX, s // 4)
    assert tk >= n_top
    qrow = lambda width: pl.BlockSpec((1, tq, width), lambda bi, i: (bi, i, 0))
    full = lambda width: _resident((1, s, width), lambda bi, i: (bi, 0, 0))
    return pl.pallas_call(
        functools.partial(_dsa_kernel, n_top=n_top),
        grid=(b, s // tq),
        in_specs=[qrow(DSA_HEADS * HEAD_DIM), full(DSA_KV_HEADS * HEAD_DIM), full(DSA_KV_HEADS * HEAD_DIM),
                  qrow(IDX_HEADS * IDX_DIM), full(LANES), full(LANES), qrow(LANES)],
        out_specs=qrow(DSA_HEADS * HEAD_DIM),
        out_shape=jax.ShapeDtypeStruct((b, s, DSA_HEADS * HEAD_DIM), F32),
        scratch_shapes=[pltpu.VMEM((s, tq), F32),
                        pltpu.VMEM((DSA_HEADS // 2, 1, 2 * tq), F32),
                        pltpu.VMEM((DSA_HEADS // 2, 1, 2 * tq), F32),
                        pltpu.VMEM((DSA_HEADS // 2, HEAD_DIM, 2 * tq), F32),
                        pltpu.VMEM((DSA_HEADS // 2, tk, 2 * tq), F32), pltpu.VMEM((DSA_HEADS // 2, tk, 2 * tq), F32),
                        pltpu.VMEM((DSA_HEADS // 2, tk, 2 * tq), BF16)],
        compiler_params=_params(2),
        name="dsa_attn",
    )(qb, kb, vb, qi, klo, khi, wi)


def _rms(z, g):
    ms = jnp.mean(z * z, axis=-1, keepdims=True)
    return z * lax.rsqrt(ms + NORM_EPS) * g


def _outproj_kernel(ya_ref, yb_ref, ga_ref, gb_ref, x_ref, gt_ref, g_ref, w_ref, o_ref):
    mixed = ga_ref[0] * ya_ref[0] + gb_ref[0] * yb_ref[0]
    z = jnp.dot(mixed.astype(BF16), w_ref[...], preferred_element_type=F32)
    o_ref[0] = x_ref[0] + gt_ref[0] * _rms(z, g_ref[...])


def _outproj(ya, yb, ga, gb, x, gt, g, w, tm):
    b, s, d = x.shape
    row = pl.BlockSpec((1, tm, d), lambda bi, i: (bi, i, 0))
    return pl.pallas_call(
        _outproj_kernel,
        grid=(b, s // tm),
        in_specs=[row, row, row, row, row, pl.BlockSpec((1, 1, d), lambda bi, i: (bi, 0, 0)),
                  pl.BlockSpec((1, d), lambda bi, i: (0, 0)), _resident((d, d), lambda bi, i: (0, 0))],
        out_specs=row,
        out_shape=jax.ShapeDtypeStruct((b, s, d), F32),
        compiler_params=_params(2),
        name="out_proj",
    )(ya, yb, ga, gb, x, gt, g.reshape(1, d), w)


_FF_TILE = 256


def _ffn_kernel(x_ref, sc_ref, sh_ref, gt_ref, gpre_ref, gpost_ref, wg_ref, wu_ref, wd_ref, o_ref, a_ref):
    x = x_ref[0]
    h = (_rms(x, gpre_ref[...]) * (1.0 + sc_ref[0]) + sh_ref[0]).astype(BF16)
    for c0 in range(0, D_FF, _FF_TILE):
        gte = jnp.dot(h, wg_ref[:, c0:c0 + _FF_TILE], preferred_element_type=F32)
        up = jnp.dot(h, wu_ref[:, c0:c0 + _FF_TILE], preferred_element_type=F32)
        a_ref[:, c0:c0 + _FF_TILE] = (gte * jax.nn.sigmoid(gte) * up).astype(BF16)
    f = jnp.dot(a_ref[...], wd_ref[...], preferred_element_type=F32)
    o_ref[0] = x + gt_ref[0] * _rms(f, gpost_ref[...])


def _ffn(x, sc, sh, gt, gpre, gpost, wg, wu, wd, tm):
    b, s, d = x.shape
    ff = wg.shape[1]
    row = pl.BlockSpec((1, tm, d), lambda bi, i: (bi, i, 0))
    mod = pl.BlockSpec((1, 1, d), lambda bi, i: (bi, 0, 0))
    gain = pl.BlockSpec((1, d), lambda bi, i: (0, 0))
    return pl.pallas_call(
        _ffn_kernel,
        grid=(b, s // tm),
        in_specs=[row, mod, mod, mod, gain, gain,
                  _resident((d, ff), lambda bi, i: (0, 0)), _resident((d, ff), lambda bi, i: (0, 0)),
                  _resident((ff, d), lambda bi, i: (0, 0))],
        out_specs=row,
        out_shape=jax.ShapeDtypeStruct((b, s, d), F32),
        scratch_shapes=[pltpu.VMEM((tm, ff), BF16)],
        compiler_params=_params(2),
        name="ffn",
    )(x, sc, sh, gt, gpre.reshape(1, d), gpost.reshape(1, d), wg, wu, wd)


def kernel(x, c, positions, w_ada, b_ada, g_pre_mix, g_post_mix, w_in, w_out, g_pre_ffn, g_post_ffn,
           w_gate, w_up, w_down):
    b, s, d = x.shape
    depth = w_ada.shape[0]
    tm = min(s, 256)

    c_pad = jnp.zeros((8, d), F32).at[:b].set(c)
    mod = _ada(c_pad, w_ada, b_ada)[:, :b]
    tabs = _rope_tables(positions)

    for l in range(depth):
        sh1, sc1, gt1, sh2, sc2, gt2 = [mod[l, :, j * d:(j + 1) * d].reshape(b, 1, d) for j in range(N_MOD)]
        pr = _inproj(x, sc1, sh1, g_pre_mix[l], _arrange_w_in(w_in[l]), tabs, tm)
        ya = _moba(pr["qa"], pr["ka"], pr["va"])
        yb = _dsa(pr["qb"], pr["kb"], pr["vb"], pr["qi"], pr["klo"], pr["khi"], pr["wi"])
        x = _outproj(ya, yb, pr["ga"], pr["gb"], x, gt1, g_post_mix[l], w_out[l].astype(BF16), tm)
        x = _ffn(x, sc2, sh2, gt2, g_pre_ffn[l], g_post_ffn[l],
                 w_gate[l].astype(BF16), w_up[l].astype(BF16), w_down[l].astype(BF16), tm)
    return x
```

```python
import functools

import jax
import jax.numpy as jnp
import numpy as np
from jax import lax
from jax.experimental import pallas as pl
from jax.experimental.pallas import tpu as pltpu

F32 = jnp.float32
BF16 = jnp.bfloat16
I32 = jnp.int32

D_MODEL = 1024
HEAD_DIM = 128
ROT_DIM = HEAD_DIM // 4
ROPE_THETA = 500000.0
MOBA_HEADS = 8
MOBA_BLOCK = 256
MOBA_TOPK = 3
DSA_HEADS = 8
DSA_KV_HEADS = 2
DSA_GROUP = DSA_HEADS // DSA_KV_HEADS
DSA_TOPK_MAX = 256
IDX_HEADS = 8
IDX_DIM = 64
IDX_ROT_DIM = IDX_DIM // 4
D_FF = 2816
N_MOD = 6
NORM_EPS = 1e-6
NEG_INF = -1e30

LANES = 128
VMEM_LIMIT = 56 * 1024 * 1024

_OFF_QA, _OFF_KA, _OFF_VA, _OFF_QB = 0, 1024, 2048, 3072
_OFF_KB, _OFF_VB, _OFF_QI, _OFF_KI, _OFF_WI, _OFF_GA, _OFF_GB = 4096, 4352, 4608, 5120, 5184, 5192, 6216

INT_MIN = -2147483648
INT_MAX = 2147483647


def _params(n_axes):
    return pltpu.CompilerParams(dimension_semantics=("arbitrary",) * n_axes,
                                vmem_limit_bytes=VMEM_LIMIT)


def _resident(block_shape, index_map):
    return pl.BlockSpec(block_shape, index_map, pipeline_mode=pl.Buffered(1))


def _ada_kernel(c_ref, w_ref, b_ref, o_ref):
    c = c_ref[...]
    s = c * jax.nn.sigmoid(c)
    o_ref[0] = jnp.dot(s, w_ref[0], preferred_element_type=F32) + b_ref[0]


def _ada(c_pad, w_ada, b_ada):
    depth, d, n = w_ada.shape
    tn = 1536
    rows = c_pad.shape[0]
    return pl.pallas_call(
        _ada_kernel,
        grid=(depth, n // tn),
        in_specs=[pl.BlockSpec((rows, d), lambda l, j: (0, 0)),
                  pl.BlockSpec((1, d, tn), lambda l, j: (l, 0, j)),
                  pl.BlockSpec((1, 1, tn), lambda l, j: (l, 0, j))],
        out_specs=pl.BlockSpec((1, rows, tn), lambda l, j: (l, 0, j)),
        out_shape=jax.ShapeDtypeStruct((depth, rows, n), F32),
        compiler_params=_params(2),
        name="ada_mod",
    )(c_pad, w_ada, b_ada.reshape(depth, 1, n))


def _rope_table_kernel(pos_ref, inv_ref, sgn_ref, cm_ref, sm_ref, ci_ref, si_ref):
    pos = pos_ref[0].astype(F32)
    ang_m = pos * inv_ref[0:1, :]
    ang_i = pos * inv_ref[1:2, :]
    cm_ref[0] = jnp.cos(ang_m)
    sm_ref[0] = jnp.sin(ang_m) * sgn_ref[0:1, :]
    ci_ref[0] = jnp.cos(ang_i)
    si_ref[0] = jnp.sin(ang_i) * sgn_ref[1:2, :]


def _rope_lane_constants():
    lane = np.arange(LANES)
    half_m, half_i = ROT_DIM // 2, IDX_ROT_DIM // 2
    inv_m = ROPE_THETA ** (-jnp.arange(half_m, dtype=F32) * 2.0 / ROT_DIM)
    inv_i = ROPE_THETA ** (-jnp.arange(half_i, dtype=F32) * 2.0 / IDX_ROT_DIM)
    in_m = lane < ROT_DIM
    in_i = (lane % IDX_DIM) < IDX_ROT_DIM
    inv = jnp.stack([jnp.where(in_m, inv_m[lane % half_m], 0.0),
                     jnp.where(in_i, inv_i[lane % half_i], 0.0)]).astype(F32)
    sgn_m = np.where(lane < half_m, -1.0, 1.0)
    sgn_i = np.where((lane % IDX_DIM) < half_i, -1.0, 1.0)
    sgn = jnp.asarray(np.stack([sgn_m, sgn_i]), F32)
    return inv, sgn


def _rope_tables(positions):
    b, s = positions.shape
    tm = min(s, 1024)
    inv, sgn = _rope_lane_constants()
    tab = jax.ShapeDtypeStruct((b, s, LANES), F32)
    tspec = pl.BlockSpec((1, tm, LANES), lambda bi, i: (bi, i, 0))
    cspec = pl.BlockSpec((2, LANES), lambda bi, i: (0, 0))
    return pl.pallas_call(
        _rope_table_kernel,
        grid=(b, s // tm),
        in_specs=[pl.BlockSpec((1, tm, 1), lambda bi, i: (bi, i, 0)), cspec, cspec],
        out_specs=[tspec] * 4,
        out_shape=[tab] * 4,
        compiler_params=_params(2),
        name="rope_tables",
    )(positions.reshape(b, s, 1), inv, sgn)


def _rope(t, cos, sin, period, half):
    lane = lax.broadcasted_iota(I32, t.shape, 1)
    first = (lane % period) < half
    partner = jnp.where(first, pltpu.roll(t, LANES - half, 1), pltpu.roll(t, half, 1))
    return t * cos + partner * sin


_IN_TILE = 256
QK_EXP2_SCALE = (HEAD_DIM ** -0.5) * 1.4426950408889634


def _inproj_kernel(x_ref, sc_ref, sh_ref, g_ref, w_ref, cm_ref, sm_ref, ci_ref, si_ref,
                   qa_ref, ka_ref, va_ref, qb_ref, kb_ref, vb_ref, qi_ref, klo_ref, khi_ref,
                   wi_ref, ga_ref, gb_ref, h_ref):
    x = x_ref[0]
    ms = jnp.mean(x * x, axis=-1, keepdims=True)
    y = x * lax.rsqrt(ms + NORM_EPS) * g_ref[...]
    h_ref[...] = (y * (1.0 + sc_ref[0]) + sh_ref[0]).astype(BF16)

    def proj(c0, width):
        return jnp.dot(h_ref[...], w_ref[:, c0:c0 + width], preferred_element_type=F32)

    def rope_cols(t, kind):
        if kind == "main":
            cos, sin, period, half = cm_ref[0], sm_ref[0], HEAD_DIM, ROT_DIM // 2
        else:
            cos, sin, period, half = ci_ref[0], si_ref[0], IDX_DIM, IDX_ROT_DIM // 2
        slabs = [_rope(t[:, j:j + LANES], cos, sin, period, half) for j in range(0, t.shape[1], LANES)]
        return slabs[0] if len(slabs) == 1 else jnp.concatenate(slabs, axis=1)

    col = 0
    plan = [(qa_ref, 1024, "query"), (ka_ref, 1024, "main"), (va_ref, 1024, None), (qb_ref, 1024, "query"),
            (kb_ref, 256, "main"), (vb_ref, 256, None), (qi_ref, 512, "idx"),
            (klo_ref, 128, "idx"), (khi_ref, 128, "idx"), (wi_ref, 128, "wi"),
            (ga_ref, 1024, "gate"), (gb_ref, 1024, "gate")]
    for ref, width, kind in plan:
        for c0 in range(0, width, _IN_TILE):
            wd = min(_IN_TILE, width - c0)
            t = proj(col + c0, wd)
            if kind in ("main", "idx"):
                t = rope_cols(t, kind)
            elif kind == "query":
                t = rope_cols(t, "main") * QK_EXP2_SCALE
            elif kind == "wi":
                t = t * (IDX_HEADS ** -0.5)
            elif kind == "gate":
                t = jax.nn.sigmoid(t)
            if any(ref is r for r in (qa_ref, ka_ref, va_ref)):
                for j in range(wd // HEAD_DIM):
                    ref[0, c0 // HEAD_DIM + j] = t[:, j * HEAD_DIM:(j + 1) * HEAD_DIM].astype(ref.dtype)
            else:
                ref[0, :, c0:c0 + wd] = t.astype(ref.dtype)
        col += width


def _arrange_w_in(w_in_l):
    d = w_in_l.shape[0]
    w_in_l = w_in_l.astype(BF16)
    ki = w_in_l[:, _OFF_KI:_OFF_KI + IDX_DIM]
    z64 = jnp.zeros((d, IDX_DIM), w_in_l.dtype)
    wi = w_in_l[:, _OFF_WI:_OFF_WI + IDX_HEADS]
    zwi = jnp.zeros((d, LANES - IDX_HEADS), w_in_l.dtype)
    parts = [w_in_l[:, :_OFF_KI], ki, z64, z64, ki, wi, zwi, w_in_l[:, _OFF_GA:]]
    return jnp.concatenate(parts, axis=1)


def _inproj(x, sc, sh, g, w_arr, tabs, tm):
    b, s, d = x.shape
    n = w_arr.shape[1]
    row = lambda width: pl.BlockSpec((1, tm, width), lambda bi, i: (bi, i, 0))
    mod = pl.BlockSpec((1, 1, d), lambda bi, i: (bi, 0, 0))
    heads = pl.BlockSpec((1, MOBA_HEADS, tm, HEAD_DIM), lambda bi, i: (bi, 0, i, 0))
    head_major = ("qa", "ka", "va")
    outs = [("qa", 1024, BF16), ("ka", 1024, BF16), ("va", 1024, BF16), ("qb", 1024, BF16),
            ("kb", 256, BF16), ("vb", 256, BF16), ("qi", 512, BF16), ("klo", 128, BF16), ("khi", 128, BF16),
            ("wi", 128, F32), ("ga", 1024, F32), ("gb", 1024, F32)]
    res = pl.pallas_call(
        _inproj_kernel,
        grid=(b, s // tm),
        in_specs=[row(d), mod, mod, pl.BlockSpec((1, d), lambda bi, i: (0, 0)),
                  _resident((d, n), lambda bi, i: (0, 0)),
                  row(LANES), row(LANES), row(LANES), row(LANES)],
        out_specs=[heads if nm in head_major else row(wd) for nm, wd, _ in outs],
        out_shape=[jax.ShapeDtypeStruct((b, MOBA_HEADS, s, HEAD_DIM) if nm in head_major else (b, s, wd), dt)
                   for nm, wd, dt in outs],
        scratch_shapes=[pltpu.VMEM((tm, d), BF16)],
        compiler_params=_params(2),
        name="in_proj",
    )(x, sc, sh, g.reshape(1, d), w_arr, *tabs)
    return dict(zip([nm for nm, _, _ in outs], res))


_NT = (((1,), (1,)), ((), ()))
_TN = (((0,), (0,)), ((), ()))


def _moba_kernel(q_ref, k_ref, v_ref, o_ref, kmean_ref, bias_ref, m_ref, l_ref, acc_ref, s0_ref, s1_ref,
                 *, nb, nbp):
    i = pl.program_id(1)
    tq = MOBA_BLOCK

    @pl.when(i == 0)
    def _():
        kmean_ref[...] = jnp.zeros_like(kmean_ref)

        def mean_body(j, carry):
            for h in range(MOBA_HEADS):
                kb = k_ref[0, h, pl.ds(pl.multiple_of(j * tq, tq), tq), :].astype(F32)
                kmean_ref[h, pl.ds(j, 1), :] = jnp.sum(kb, axis=0, keepdims=True) * (1.0 / tq)
            return carry

        lax.fori_loop(0, nb, mean_body, 0)

    off_own = pl.multiple_of(i * tq, tq)
    blk = lax.broadcasted_iota(I32, (nbp, tq), 0)
    kr = lax.broadcasted_iota(I32, (tq, tq), 0)
    qc = lax.broadcasted_iota(I32, (tq, tq), 1)
    for h in range(MOBA_HEADS):
        hs = slice(h * HEAD_DIM, (h + 1) * HEAD_DIM)
        q = q_ref[0, h]
        km = kmean_ref[h, 0:nbp, :].astype(BF16)
        gate = lax.dot_general(km, q, _NT, preferred_element_type=F32)
        gate = jnp.where(blk < i, gate, NEG_INF)
        chosen = jnp.zeros(gate.shape, F32)
        for _ in range(MOBA_TOPK):
            mx = jnp.max(gate, axis=0, keepdims=True)
            first = jnp.min(jnp.where(gate == mx, blk, LANES), axis=0, keepdims=True)
            hit = blk == first
            chosen = jnp.where(hit, 1.0, chosen)
            gate = jnp.where(hit, -jnp.inf, gate)
        bias_ref[h] = jnp.where(blk < i, jnp.where(chosen > 0.0, 0.0, -jnp.inf), -jnp.inf)

        s = lax.dot_general(k_ref[0, h, pl.ds(off_own, tq), :], q, _NT, preferred_element_type=F32)
        s = jnp.where(kr <= qc, s, -jnp.inf)
        m0 = jnp.max(s, axis=0, keepdims=True)
        p = jnp.exp2(s - m0)
        m_ref[h] = m0
        l_ref[h] = jnp.sum(p, axis=0, keepdims=True)
        acc_ref[h] = lax.dot_general(v_ref[0, h, pl.ds(off_own, tq), :], p.astype(BF16), _TN,
                                     preferred_element_type=F32)

    def stage_a(j, s_ref):
        off = pl.multiple_of(j * tq, tq)
        for h in range(MOBA_HEADS):
            hs = slice(h * HEAD_DIM, (h + 1) * HEAD_DIM)
            s_ref[h] = lax.dot_general(k_ref[0, h, pl.ds(off, tq), :], q_ref[0, h], _NT,
                                       preferred_element_type=F32)

    def stage_b(j, s_ref):
        off = pl.multiple_of(j * tq, tq)
        for h in range(MOBA_HEADS):
            hs = slice(h * HEAD_DIM, (h + 1) * HEAD_DIM)
            s = s_ref[h]
            gate_row = bias_ref[h, pl.ds(j, 1), :]
            m_old = m_ref[h]
            m_new = jnp.maximum(m_old, jnp.max(s, axis=0, keepdims=True) + gate_row)
            alpha = jnp.exp2(m_old - m_new)
            p = jnp.exp2(s - (m_new - gate_row))
            l_ref[h] = alpha * l_ref[h] + jnp.sum(p, axis=0, keepdims=True)
            acc_ref[h] = alpha * acc_ref[h] + lax.dot_general(
                v_ref[0, h, pl.ds(off, tq), :], p.astype(BF16), _TN, preferred_element_type=F32)
            m_ref[h] = m_new

    stage_a(0, s0_ref)

    def pair_body(t, carry):
        j = 2 * t
        stage_a(j + 1, s1_ref)
        stage_b(j, s0_ref)
        stage_a(j + 2, s0_ref)
        stage_b(j + 1, s1_ref)
        return carry

    lax.fori_loop(0, i // 2, pair_body, 0)

    @pl.when(i % 2 == 1)
    def _():
        stage_b(i - 1, s0_ref)

    for h in range(MOBA_HEADS):
        o_ref[0, :, h * HEAD_DIM:(h + 1) * HEAD_DIM] = (acc_ref[h] / l_ref[h]).T


def _moba(qa, ka, va):
    b, nh, s, dh = qa.shape
    w = nh * dh
    tq = MOBA_BLOCK
    nb = s // tq
    nbp = -(-nb // 8) * 8
    assert s % tq == 0 and nbp <= LANES
    return pl.pallas_call(
        functools.partial(_moba_kernel, nb=nb, nbp=nbp),
        grid=(b, nb),
        in_specs=[pl.BlockSpec((1, nh, tq, dh), lambda bi, i: (bi, 0, i, 0)),
                  _resident((1, nh, s, dh), lambda bi, i: (bi, 0, 0, 0)),
                  _resident((1, nh, s, dh), lambda bi, i: (bi, 0, 0, 0))],
        out_specs=pl.BlockSpec((1, tq, w), lambda bi, i: (bi, i, 0)),
        out_shape=jax.ShapeDtypeStruct((b, s, w), F32),
        scratch_shapes=[pltpu.VMEM((MOBA_HEADS, LANES, HEAD_DIM), F32),
                        pltpu.VMEM((MOBA_HEADS, nbp, tq), F32),
                        pltpu.VMEM((MOBA_HEADS, 1, tq), F32), pltpu.VMEM((MOBA_HEADS, 1, tq), F32),
                        pltpu.VMEM((MOBA_HEADS, HEAD_DIM, tq), F32),
                        pltpu.VMEM((MOBA_HEADS, tq, tq), F32), pltpu.VMEM((MOBA_HEADS, tq, tq), F32)],
        compiler_params=_params(2),
        name="moba_attn",
    )(qa, ka, va)


_DSA_TQ = 128
_DSA_TK = 512
_DSA_SLAB = 32


def _dsa_kernel(qb_ref, kb_ref, vb_ref, qi_ref, klo_ref, khi_ref, wi_ref, o_ref,
                sc_ref, m_ref, l_ref, acc_ref, s0_ref, s1_ref, p_ref, *, n_top):
    i = pl.program_id(1)
    tq, tk, slab = _DSA_TQ, _DSA_TK, _DSA_SLAB
    n_chunks = (i * tq + tq + tk - 1) // tk
    f32_lowest = float(np.finfo(np.float32).min)

    kr = lax.broadcasted_iota(I32, (tk, tq), 0)
    qpos = i * tq + lax.broadcasted_iota(I32, (tk, tq), 1)

    wt = (wi_ref[0] * (IDX_DIM ** -0.5)).T
    wrow = [wt[h:h + 1, :] for h in range(IDX_HEADS)]
    pairs = IDX_HEADS // 2
    qi2 = [jnp.concatenate([qi_ref[0, :, p * LANES:(p + 1) * LANES] for p in (2 * u, 2 * u + 1)], axis=0)
           for u in range(pairs // 2)]

    def idx_body(c, carry):
        off = pl.multiple_of(c * tk, tk)
        units = [(u, par) for u in range(pairs // 2) for par in (0, 1)]

        def logits(u, par):
            kref = klo_ref if par == 0 else khi_ref
            return lax.dot_general(kref[0, pl.ds(off, tk), :], qi2[u], _NT, preferred_element_type=F32)

        sc = jnp.zeros((tk, tq), F32)
        nxt = logits(*units[0])
        for n, (u, par) in enumerate(units):
            lg = nxt
            if n + 1 < len(units):
                nxt = logits(*units[n + 1])
            for j in range(2):
                h = 2 * (2 * u + j) + par
                sc = sc + wrow[h] * jnp.maximum(lg[:, j * tq:(j + 1) * tq], 0.0)
        sc_ref[pl.ds(off, tk), :] = jnp.where(c * tk + kr <= qpos, sc, -jnp.inf)
        return carry

    lax.fori_loop(0, n_chunks, idx_body, 0)

    def float_of(u):
        ci = u ^ INT_MIN
        return lax.bitcast_convert_type(ci ^ ((ci >> 31) & INT_MAX), F32)

    srow = lax.broadcasted_iota(I32, (slab, tq), 0)

    def count(pred):
        def body(c, acc):
            for r in range(tk // slab):
                off = pl.multiple_of(c * tk + r * slab, slab)
                acc = acc + jnp.where(pred(sc_ref[pl.ds(off, slab), :], off + srow), 1, 0)
            return acc
        acc = lax.fori_loop(0, n_chunks, body, jnp.zeros((slab, tq), I32))
        return jnp.sum(acc, axis=0, keepdims=True)

    total = n_chunks * tk

    def count_ge(cf):
        return total - count(lambda s, kp: s < cf)

    def bit_body(b, carry):
        u, cnt = carry
        cand = u | lax.shift_left(jnp.int32(1), 31 - b)
        cc = count_ge(float_of(cand))
        ok = cc >= n_top
        return jnp.where(ok, cand, u), jnp.where(ok, cc, cnt)

    u, cnt = lax.fori_loop(0, 32, bit_body, (jnp.zeros((1, tq), I32), jnp.full((1, tq), total, I32)))
    thr = jnp.maximum(float_of(u), f32_lowest)

    tie = jnp.logical_and(cnt > n_top, float_of(u) >= f32_lowest)
    any_tie = jnp.max(jnp.where(tie, 1, 0)) > 0

    @pl.when(any_tie)
    def _():
        need = n_top - count(lambda s, kp: s > thr)
        keep_below = jnp.where(tie, need, INT_MAX).astype(F32)
        before = (lax.broadcasted_iota(I32, (tk, tk), 1) < lax.broadcasted_iota(I32, (tk, tk), 0))
        lower = jnp.where(before, 1.0, 0.0).astype(BF16)

        def drop_body(c, seen):
            off = pl.multiple_of(c * tk, tk)
            sc = sc_ref[pl.ds(off, tk), :]
            tied = jnp.where(sc == thr, 1.0, 0.0)
            rank = jnp.dot(lower, tied.astype(BF16), preferred_element_type=F32) + seen
            drop = tied * jnp.where(rank >= keep_below, 1.0, 0.0)
            sc_ref[pl.ds(off, tk), :] = jnp.where(drop > 0.0, -jnp.inf, sc)
            return seen + jnp.sum(tied, axis=0, keepdims=True)

        lax.fori_loop(0, n_chunks, drop_body, jnp.zeros((1, tq), F32))

    m_ref[...] = jnp.full(m_ref.shape, NEG_INF, F32)
    l_ref[...] = jnp.zeros(l_ref.shape, F32)
    acc_ref[...] = jnp.zeros(acc_ref.shape, F32)
    n_units = DSA_HEADS // 2
    q2 = [jnp.concatenate([qb_ref[0, :, h * HEAD_DIM:(h + 1) * HEAD_DIM] for h in (2 * u, 2 * u + 1)], axis=0)
          for u in range(n_units)]

    def kv_cols(u):
        g = (2 * u) // DSA_GROUP
        return slice(g * HEAD_DIM, (g + 1) * HEAD_DIM)

    all_units = tuple(range(n_units))

    def selection_bias(c):
        off = pl.multiple_of(c * tk, tk)
        bias = jnp.where(sc_ref[pl.ds(off, tk), :] >= thr, 0.0, -jnp.inf)
        return jnp.concatenate([bias, bias], axis=1)

    def stage_a(c, s_ref, bias2, units=all_units):
        off = pl.multiple_of(c * tk, tk)
        for u in units:
            s_ref[u] = bias2 + lax.dot_general(kb_ref[0, pl.ds(off, tk), kv_cols(u)], q2[u], _NT,
                                               preferred_element_type=F32)

    def stage_b(c, s_ref, units=all_units):
        off = pl.multiple_of(c * tk, tk)
        for u in units:
            mx = s_ref[u, 0:slab, :]
            for r in range(1, tk // slab):
                mx = jnp.maximum(mx, s_ref[u, r * slab:(r + 1) * slab, :])
            m_old = m_ref[u]
            m_new = jnp.maximum(m_old, jnp.max(mx, axis=0, keepdims=True))
            alpha = jnp.exp2(m_old - m_new)
            part = jnp.zeros((slab, 2 * tq), F32)
            for r in range(tk // slab):
                p = jnp.exp2(s_ref[u, r * slab:(r + 1) * slab, :] - m_new)
                part = part + p
                p_ref[u, r * slab:(r + 1) * slab, :] = p.astype(BF16)
            l_ref[u] = alpha * l_ref[u] + jnp.sum(part, axis=0, keepdims=True)
            acc_ref[u] = alpha * acc_ref[u] + lax.dot_general(
                vb_ref[0, pl.ds(off, tk), kv_cols(u)], p_ref[u], _TN, preferred_element_type=F32)
            m_ref[u] = m_new

    def stage_ab(ca, sa_ref, cb, sb_ref):
        bias2 = selection_bias(ca)
        for u in all_units:
            stage_a(ca, sa_ref, bias2, units=(u,))
            stage_b(cb, sb_ref, units=(u,))

    stage_a(0, s0_ref, selection_bias(0))

    def pair_body(t, carry):
        c = 2 * t
        stage_ab(c + 1, s1_ref, c, s0_ref)
        stage_ab(jnp.minimum(c + 2, n_chunks - 1), s0_ref, c + 1, s1_ref)
        return carry

    lax.fori_loop(0, n_chunks // 2, pair_body, 0)

    @pl.when(n_chunks % 2 == 1)
    def _():
        stage_b(n_chunks - 1, s0_ref)

    for u in range(n_units):
        o_t = acc_ref[u] / l_ref[u]
        for j in range(2):
            h = 2 * u + j
            o_ref[0, :, h * HEAD_DIM:(h + 1) * HEAD_DIM] = o_t[:, j * tq:(j + 1) * tq].T


def _dsa(qb, kb, vb, qi, klo, khi, wi):
    b, s, _ = qb.shape
    tq, tk = _DSA_TQ, _DSA_TK
    assert s % tk == 0
    n_top = min(DSA_TOPK_MAX, s // 4)
    assert tk >= n_top
    qrow = lambda width: pl.BlockSpec((1, tq, width), lambda bi, i: (bi, i, 0))
    full = lambda width: _resident((1, s, width), lambda bi, i: (bi, 0, 0))
    return pl.pallas_call(
        functools.partial(_dsa_kernel, n_top=n_top),
        grid=(b, s // tq),
        in_specs=[qrow(DSA_HEADS * HEAD_DIM), full(DSA_KV_HEADS * HEAD_DIM), full(DSA_KV_HEADS * HEAD_DIM),
                  qrow(IDX_HEADS * IDX_DIM), full(LANES), full(LANES), qrow(LANES)],
        out_specs=qrow(DSA_HEADS * HEAD_DIM),
        out_shape=jax.ShapeDtypeStruct((b, s, DSA_HEADS * HEAD_DIM), F32),
        scratch_shapes=[pltpu.VMEM((s, tq), F32),
                        pltpu.VMEM((DSA_HEADS // 2, 1, 2 * tq), F32),
                        pltpu.VMEM((DSA_HEADS // 2, 1, 2 * tq), F32),
                        pltpu.VMEM((DSA_HEADS // 2, HEAD_DIM, 2 * tq), F32),
                        pltpu.VMEM((DSA_HEADS // 2, tk, 2 * tq), F32), pltpu.VMEM((DSA_HEADS // 2, tk, 2 * tq), F32),
                        pltpu.VMEM((DSA_HEADS // 2, tk, 2 * tq), BF16)],
        compiler_params=_params(2),
        name="dsa_attn",
    )(qb, kb, vb, qi, klo, khi, wi)


def _rms(z, g):
    ms = jnp.mean(z * z, axis=-1, keepdims=True)
    return z * lax.rsqrt(ms + NORM_EPS) * g


def _outproj_kernel(ya_ref, yb_ref, ga_ref, gb_ref, x_ref, gt_ref, g_ref, w_ref, o_ref):
    mixed = ga_ref[0] * ya_ref[0] + gb_ref[0] * yb_ref[0]
    z = jnp.dot(mixed.astype(BF16), w_ref[...], preferred_element_type=F32)
    o_ref[0] = x_ref[0] + gt_ref[0] * _rms(z, g_ref[...])


def _outproj(ya, yb, ga, gb, x, gt, g, w, tm):
    b, s, d = x.shape
    row = pl.BlockSpec((1, tm, d), lambda bi, i: (bi, i, 0))
    return pl.pallas_call(
        _outproj_kernel,
        grid=(b, s // tm),
        in_specs=[row, row, row, row, row, pl.BlockSpec((1, 1, d), lambda bi, i: (bi, 0, 0)),
                  pl.BlockSpec((1, d), lambda bi, i: (0, 0)), _resident((d, d), lambda bi, i: (0, 0))],
        out_specs=row,
        out_shape=jax.ShapeDtypeStruct((b, s, d), F32),
        compiler_params=_params(2),
        name="out_proj",
    )(ya, yb, ga, gb, x, gt, g.reshape(1, d), w)


_FF_TILE = 256


def _ffn_kernel(x_ref, sc_ref, sh_ref, gt_ref, gpre_ref, gpost_ref, wg_ref, wu_ref, wd_ref, o_ref, a_ref):
    x = x_ref[0]
    h = (_rms(x, gpre_ref[...]) * (1.0 + sc_ref[0]) + sh_ref[0]).astype(BF16)
    for c0 in range(0, D_FF, _FF_TILE):
        gte = jnp.dot(h, wg_ref[:, c0:c0 + _FF_TILE], preferred_element_type=F32)
        up = jnp.dot(h, wu_ref[:, c0:c0 + _FF_TILE], preferred_element_type=F32)
        a_ref[:, c0:c0 + _FF_TILE] = (gte * jax.nn.sigmoid(gte) * up).astype(BF16)
    f = jnp.dot(a_ref[...], wd_ref[...], preferred_element_type=F32)
    o_ref[0] = x + gt_ref[0] * _rms(f, gpost_ref[...])


def _ffn(x, sc, sh, gt, gpre, gpost, wg, wu, wd, tm):
    b, s, d = x.shape
    ff = wg.shape[1]
    row = pl.BlockSpec((1, tm, d), lambda bi, i: (bi, i, 0))
    mod = pl.BlockSpec((1, 1, d), lambda bi, i: (bi, 0, 0))
    gain = pl.BlockSpec((1, d), lambda bi, i: (0, 0))
    return pl.pallas_call(
        _ffn_kernel,
        grid=(b, s // tm),
        in_specs=[row, mod, mod, mod, gain, gain,
                  _resident((d, ff), lambda bi, i: (0, 0)), _resident((d, ff), lambda bi, i: (0, 0)),
                  _resident((ff, d), lambda bi, i: (0, 0))],
        out_specs=row,
        out_shape=jax.ShapeDtypeStruct((b, s, d), F32),
        scratch_shapes=[pltpu.VMEM((tm, ff), BF16)],
        compiler_params=_params(2),
        name="ffn",
    )(x, sc, sh, gt, gpre.reshape(1, d), gpost.reshape(1, d), wg, wu, wd)


def kernel(x, c, positions, w_ada, b_ada, g_pre_mix, g_post_mix, w_in, w_out, g_pre_ffn, g_post_ffn,
           w_gate, w_up, w_down):
    b, s, d = x.shape
    depth = w_ada.shape[0]
    tm = min(s, 256)

    c_pad = jnp.zeros((8, d), F32).at[:b].set(c)
    mod = _ada(c_pad, w_ada, b_ada)[:, :b]
    tabs = _rope_tables(positions)

    for l in range(depth):
        sh1, sc1, gt1, sh2, sc2, gt2 = [mod[l, :, j * d:(j + 1) * d].reshape(b, 1, d) for j in range(N_MOD)]
        pr = _inproj(x, sc1, sh1, g_pre_mix[l], _arrange_w_in(w_in[l]), tabs, tm)
        ya = _moba(pr["qa"], pr["ka"], pr["va"])
        yb = _dsa(pr["qb"], pr["kb"], pr["vb"], pr["qi"], pr["klo"], pr["khi"], pr["wi"])
        x = _outproj(ya, yb, pr["ga"], pr["gb"], x, gt1, g_post_mix[l], w_out[l].astype(BF16), tm)
        x = _ffn(x, sc2, sh2, gt2, g_pre_ffn[l], g_post_ffn[l],
                 w_gate[l].astype(BF16), w_up[l].astype(BF16), w_down[l].astype(BF16), tm)
    return x
```

```python
import functools

import jax
import jax.numpy as jnp
import numpy as np
from jax import lax
from jax.experimental import pallas as pl
from jax.experimental.pallas import tpu as pltpu

F32 = jnp.float32
BF16 = jnp.bfloat16
I32 = jnp.int32

D_MODEL = 1024
HEAD_DIM = 128
ROT_DIM = HEAD_DIM // 4
ROPE_THETA = 500000.0
MOBA_HEADS = 8
MOBA_BLOCK = 256
MOBA_TOPK = 3
DSA_HEADS = 8
DSA_KV_HEADS = 2
DSA_GROUP = DSA_HEADS // DSA_KV_HEADS
DSA_TOPK_MAX = 256
IDX_HEADS = 8
IDX_DIM = 64
IDX_ROT_DIM = IDX_DIM // 4
D_FF = 2816
N_MOD = 6
NORM_EPS = 1e-6
NEG_INF = -1e30

LANES = 128
VMEM_LIMIT = 56 * 1024 * 1024

_OFF_QA, _OFF_KA, _OFF_VA, _OFF_QB = 0, 1024, 2048, 3072
_OFF_KB, _OFF_VB, _OFF_QI, _OFF_KI, _OFF_WI, _OFF_GA, _OFF_GB = 4096, 4352, 4608, 5120, 5184, 5192, 6216

INT_MIN = -2147483648
INT_MAX = 2147483647


def _params(n_axes):
    return pltpu.CompilerParams(dimension_semantics=("arbitrary",) * n_axes,
                                vmem_limit_bytes=VMEM_LIMIT)


def _resident(block_shape, index_map):
    return pl.BlockSpec(block_shape, index_map, pipeline_mode=pl.Buffered(1))


def _ada_kernel(c_ref, w_ref, b_ref, o_ref):
    c = c_ref[...]
    s = c * jax.nn.sigmoid(c)
    o_ref[0] = jnp.dot(s, w_ref[0], preferred_element_type=F32) + b_ref[0]


def _ada(c_pad, w_ada, b_ada):
    depth, d, n = w_ada.shape
    tn = 1536
    rows = c_pad.shape[0]
    return pl.pallas_call(
        _ada_kernel,
        grid=(depth, n // tn),
        in_specs=[pl.BlockSpec((rows, d), lambda l, j: (0, 0)),
                  pl.BlockSpec((1, d, tn), lambda l, j: (l, 0, j)),
                  pl.BlockSpec((1, 1, tn), lambda l, j: (l, 0, j))],
        out_specs=pl.BlockSpec((1, rows, tn), lambda l, j: (l, 0, j)),
        out_shape=jax.ShapeDtypeStruct((depth, rows, n), F32),
        compiler_params=_params(2),
        name="ada_mod",
    )(c_pad, w_ada, b_ada.reshape(depth, 1, n))


def _rope_table_kernel(pos_ref, inv_ref, sgn_ref, cm_ref, sm_ref, ci_ref, si_ref):
    pos = pos_ref[0].astype(F32)
    ang_m = pos * inv_ref[0:1, :]
    ang_i = pos * inv_ref[1:2, :]
    cm_ref[0] = jnp.cos(ang_m)
    sm_ref[0] = jnp.sin(ang_m) * sgn_ref[0:1, :]
    ci_ref[0] = jnp.cos(ang_i)
    si_ref[0] = jnp.sin(ang_i) * sgn_ref[1:2, :]


def _rope_lane_constants():
    lane = np.arange(LANES)
    half_m, half_i = ROT_DIM // 2, IDX_ROT_DIM // 2
    inv_m = ROPE_THETA ** (-jnp.arange(half_m, dtype=F32) * 2.0 / ROT_DIM)
    inv_i = ROPE_THETA ** (-jnp.arange(half_i, dtype=F32) * 2.0 / IDX_ROT_DIM)
    in_m = lane < ROT_DIM
    in_i = (lane % IDX_DIM) < IDX_ROT_DIM
    inv = jnp.stack([jnp.where(in_m, inv_m[lane % half_m], 0.0),
                     jnp.where(in_i, inv_i[lane % half_i], 0.0)]).astype(F32)
    sgn_m = np.where(lane < half_m, -1.0, 1.0)
    sgn_i = np.where((lane % IDX_DIM) < half_i, -1.0, 1.0)
    sgn = jnp.asarray(np.stack([sgn_m, sgn_i]), F32)
    return inv, sgn


def _rope_tables(positions):
    b, s = positions.shape
    tm = min(s, 1024)
    inv, sgn = _rope_lane_constants()
    tab = jax.ShapeDtypeStruct((b, s, LANES), F32)
    tspec = pl.BlockSpec((1, tm, LANES), lambda bi, i: (bi, i, 0))
    cspec = pl.BlockSpec((2, LANES), lambda bi, i: (0, 0))
    return pl.pallas_call(
        _rope_table_kernel,
        grid=(b, s // tm),
        in_specs=[pl.BlockSpec((1, tm, 1), lambda bi, i: (bi, i, 0)), cspec, cspec],
        out_specs=[tspec] * 4,
        out_shape=[tab] * 4,
        compiler_params=_params(2),
        name="rope_tables",
    )(positions.reshape(b, s, 1), inv, sgn)


def _rope(t, cos, sin, period, half):
    lane = lax.broadcasted_iota(I32, t.shape, 1)
    first = (lane % period) < half
    partner = jnp.where(first, pltpu.roll(t, LANES - half, 1), pltpu.roll(t, half, 1))
    return t * cos + partner * sin


_IN_TILE = 256
QK_EXP2_SCALE = (HEAD_DIM ** -0.5) * 1.4426950408889634


def _inproj_kernel(x_ref, sc_ref, sh_ref, g_ref, w_ref, cm_ref, sm_ref, ci_ref, si_ref,
                   qa_ref, ka_ref, va_ref, qb_ref, kb_ref, vb_ref, qi_ref, klo_ref, khi_ref,
                   wi_ref, ga_ref, gb_ref, h_ref):
    x = x_ref[0]
    ms = jnp.mean(x * x, axis=-1, keepdims=True)
    y = x * lax.rsqrt(ms + NORM_EPS) * g_ref[...]
    h_ref[...] = (y * (1.0 + sc_ref[0]) + sh_ref[0]).astype(BF16)

    def proj(c0, width):
        return jnp.dot(h_ref[...], w_ref[:, c0:c0 + width], preferred_element_type=F32)

    def rope_cols(t, kind):
        if kind == "main":
            cos, sin, period, half = cm_ref[0], sm_ref[0], HEAD_DIM, ROT_DIM // 2
        else:
            cos, sin, period, half = ci_ref[0], si_ref[0], IDX_DIM, IDX_ROT_DIM // 2
        slabs = [_rope(t[:, j:j + LANES], cos, sin, period, half) for j in range(0, t.shape[1], LANES)]
        return slabs[0] if len(slabs) == 1 else jnp.concatenate(slabs, axis=1)

    col = 0
    plan = [(qa_ref, 1024, "query"), (ka_ref, 1024, "main"), (va_ref, 1024, None), (qb_ref, 1024, "query"),
            (kb_ref, 256, "main"), (vb_ref, 256, None), (qi_ref, 512, "idx"),
            (klo_ref, 128, "idx"), (khi_ref, 128, "idx"), (wi_ref, 128, "wi"),
            (ga_ref, 1024, "gate"), (gb_ref, 1024, "gate")]
    for ref, width, kind in plan:
        for c0 in range(0, width, _IN_TILE):
            wd = min(_IN_TILE, width - c0)
            t = proj(col + c0, wd)
            if kind in ("main", "idx"):
                t = rope_cols(t, kind)
            elif kind == "query":
                t = rope_cols(t, "main") * QK_EXP2_SCALE
            elif kind == "wi":
                t = t * (IDX_HEADS ** -0.5)
            elif kind == "gate":
                t = jax.nn.sigmoid(t)
            if any(ref is r for r in (qa_ref, ka_ref, va_ref)):
                for j in range(wd // HEAD_DIM):
                    ref[0, c0 // HEAD_DIM + j] = t[:, j * HEAD_DIM:(j + 1) * HEAD_DIM].astype(ref.dtype)
            else:
                ref[0, :, c0:c0 + wd] = t.astype(ref.dtype)
        col += width


def _arrange_w_in(w_in_l):
    d = w_in_l.shape[0]
    w_in_l = w_in_l.astype(BF16)
    ki = w_in_l[:, _OFF_KI:_OFF_KI + IDX_DIM]
    z64 = jnp.zeros((d, IDX_DIM), w_in_l.dtype)
    wi = w_in_l[:, _OFF_WI:_OFF_WI + IDX_HEADS]
    zwi = jnp.zeros((d, LANES - IDX_HEADS), w_in_l.dtype)
    parts = [w_in_l[:, :_OFF_KI], ki, z64, z64, ki, wi, zwi, w_in_l[:, _OFF_GA:]]
    return jnp.concatenate(parts, axis=1)


def _inproj(x, sc, sh, g, w_arr, tabs, tm):
    b, s, d = x.shape
    n = w_arr.shape[1]
    row = lambda width: pl.BlockSpec((1, tm, width), lambda bi, i: (bi, i, 0))
    mod = pl.BlockSpec((1, 1, d), lambda bi, i: (bi, 0, 0))
    heads = pl.BlockSpec((1, MOBA_HEADS, tm, HEAD_DIM), lambda bi, i: (bi, 0, i, 0))
    head_major = ("qa", "ka", "va")
    outs = [("qa", 1024, BF16), ("ka", 1024, BF16), ("va", 1024, BF16), ("qb", 1024, BF16),
            ("kb", 256, BF16), ("vb", 256, BF16), ("qi", 512, BF16), ("klo", 128, BF16), ("khi", 128, BF16),
            ("wi", 128, F32), ("ga", 1024, F32), ("gb", 1024, F32)]
    res = pl.pallas_call(
        _inproj_kernel,
        grid=(b, s // tm),
        in_specs=[row(d), mod, mod, pl.BlockSpec((1, d), lambda bi, i: (0, 0)),
                  _resident((d, n), lambda bi, i: (0, 0)),
                  row(LANES), row(LANES), row(LANES), row(LANES)],
        out_specs=[heads if nm in head_major else row(wd) for nm, wd, _ in outs],
        out_shape=[jax.ShapeDtypeStruct((b, MOBA_HEADS, s, HEAD_DIM) if nm in head_major else (b, s, wd), dt)
                   for nm, wd, dt in outs],
        scratch_shapes=[pltpu.VMEM((tm, d), BF16)],
        compiler_params=_params(2),
        name="in_proj",
    )(x, sc, sh, g.reshape(1, d), w_arr, *tabs)
    return dict(zip([nm for nm, _, _ in outs], res))


_NT = (((1,), (1,)), ((), ()))
_TN = (((0,), (0,)), ((), ()))


def _moba_kernel(q_ref, k_ref, v_ref, o_ref, kmean_ref, bias_ref, m_ref, l_ref, acc_ref, s0_ref, s1_ref,
                 *, nb, nbp):
    i = pl.program_id(1)
    tq = MOBA_BLOCK

    @pl.when(i == 0)
    def _():
        kmean_ref[...] = jnp.zeros_like(kmean_ref)

        def mean_body(j, carry):
            for h in range(MOBA_HEADS):
                kb = k_ref[0, h, pl.ds(pl.multiple_of(j * tq, tq), tq), :].astype(F32)
                kmean_ref[h, pl.ds(j, 1), :] = jnp.sum(kb, axis=0, keepdims=True) * (1.0 / tq)
            return carry

        lax.fori_loop(0, nb, mean_body, 0)

    off_own = pl.multiple_of(i * tq, tq)
    blk = lax.broadcasted_iota(I32, (nbp, tq), 0)
    kr = lax.broadcasted_iota(I32, (tq, tq), 0)
    qc = lax.broadcasted_iota(I32, (tq, tq), 1)
    for h in range(MOBA_HEADS):
        hs = slice(h * HEAD_DIM, (h + 1) * HEAD_DIM)
        q = q_ref[0, h]
        km = kmean_ref[h, 0:nbp, :].astype(BF16)
        gate = lax.dot_general(km, q, _NT, preferred_element_type=F32)
        gate = jnp.where(blk < i, gate, NEG_INF)
        chosen = jnp.zeros(gate.shape, F32)
        for _ in range(MOBA_TOPK):
            mx = jnp.max(gate, axis=0, keepdims=True)
            first = jnp.min(jnp.where(gate == mx, blk, LANES), axis=0, keepdims=True)
            hit = blk == first
            chosen = jnp.where(hit, 1.0, chosen)
            gate = jnp.where(hit, -jnp.inf, gate)
        bias_ref[h] = jnp.where(blk < i, jnp.where(chosen > 0.0, 0.0, -jnp.inf), -jnp.inf)

        s = lax.dot_general(k_ref[0, h, pl.ds(off_own, tq), :], q, _NT, preferred_element_type=F32)
        s = jnp.where(kr <= qc, s, -jnp.inf)
        m0 = jnp.max(s, axis=0, keepdims=True)
        p = jnp.exp2(s - m0)
        m_ref[h] = m0
        l_ref[h] = jnp.sum(p, axis=0, keepdims=True)
        acc_ref[h] = lax.dot_general(v_ref[0, h, pl.ds(off_own, tq), :], p.astype(BF16), _TN,
                                     preferred_element_type=F32)

    def stage_a(j, s_ref):
        off = pl.multiple_of(j * tq, tq)
        for h in range(MOBA_HEADS):
            hs = slice(h * HEAD_DIM, (h + 1) * HEAD_DIM)
            s_ref[h] = lax.dot_general(k_ref[0, h, pl.ds(off, tq), :], q_ref[0, h], _NT,
                                       preferred_element_type=F32)

    def stage_b(j, s_ref):
        off = pl.multiple_of(j * tq, tq)
        for h in range(MOBA_HEADS):
            hs = slice(h * HEAD_DIM, (h + 1) * HEAD_DIM)
            s = s_ref[h]
            gate_row = bias_ref[h, pl.ds(j, 1), :]
            m_old = m_ref[h]
            m_new = jnp.maximum(m_old, jnp.max(s, axis=0, keepdims=True) + gate_row)
            alpha = jnp.exp2(m_old - m_new)
            p = jnp.exp2(s - (m_new - gate_row))
            l_ref[h] = alpha * l_ref[h] + jnp.sum(p, axis=0, keepdims=True)
            acc_ref[h] = alpha * acc_ref[h] + lax.dot_general(
                v_ref[0, h, pl.ds(off, tq), :], p.astype(BF16), _TN, preferred_element_type=F32)
            m_ref[h] = m_new

    stage_a(0, s0_ref)

    def pair_body(t, carry):
        j = 2 * t
        stage_a(j + 1, s1_ref)
        stage_b(j, s0_ref)
        stage_a(j + 2, s0_ref)
        stage_b(j + 1, s1_ref)
        return carry

    lax.fori_loop(0, i // 2, pair_body, 0)

    @pl.when(i % 2 == 1)
    def _():
        stage_b(i - 1, s0_ref)

    for h in range(MOBA_HEADS):
        o_ref[0, :, h * HEAD_DIM:(h + 1) * HEAD_DIM] = (acc_ref[h] / l_ref[h]).T


def _moba(qa, ka, va):
    b, nh, s, dh = qa.shape
    w = nh * dh
    tq = MOBA_BLOCK
    nb = s // tq
    nbp = -(-nb // 8) * 8
    assert s % tq == 0 and nbp <= LANES
    return pl.pallas_call(
        functools.partial(_moba_kernel, nb=nb, nbp=nbp),
        grid=(b, nb),
        in_specs=[pl.BlockSpec((1, nh, tq, dh), lambda bi, i: (bi, 0, i, 0)),
                  _resident((1, nh, s, dh), lambda bi, i: (bi, 0, 0, 0)),
                  _resident((1, nh, s, dh), lambda bi, i: (bi, 0, 0, 0))],
        out_specs=pl.BlockSpec((1, tq, w), lambda bi, i: (bi, i, 0)),
        out_shape=jax.ShapeDtypeStruct((b, s, w), F32),
        scratch_shapes=[pltpu.VMEM((MOBA_HEADS, LANES, HEAD_DIM), F32),
                        pltpu.VMEM((MOBA_HEADS, nbp, tq), F32),
                        pltpu.VMEM((MOBA_HEADS, 1, tq), F32), pltpu.VMEM((MOBA_HEADS, 1, tq), F32),
                        pltpu.VMEM((MOBA_HEADS, HEAD_DIM, tq), F32),
                        pltpu.VMEM((MOBA_HEADS, tq, tq), F32), pltpu.VMEM((MOBA_HEADS, tq, tq), F32)],
        compiler_params=_params(2),
        name="moba_attn",
    )(qa, ka, va)


_DSA_TQ = 128
_DSA_TK = 512
_DSA_SLAB = 32


def _dsa_kernel(qb_ref, kb_ref, vb_ref, qi_ref, klo_ref, khi_ref, wi_ref, o_ref,
                sc_ref, m_ref, l_ref, acc_ref, s0_ref, s1_ref, p_ref, *, n_top):
    i = pl.program_id(1)
    tq, tk, slab = _DSA_TQ, _DSA_TK, _DSA_SLAB
    n_chunks = (i * tq + tq + tk - 1) // tk
    f32_lowest = float(np.finfo(np.float32).min)

    kr = lax.broadcasted_iota(I32, (tk, tq), 0)
    qpos = i * tq + lax.broadcasted_iota(I32, (tk, tq), 1)

    wt = (wi_ref[0] * (IDX_DIM ** -0.5)).T
    wrow = [wt[h:h + 1, :] for h in range(IDX_HEADS)]
    pairs = IDX_HEADS // 2
    qi2 = [jnp.concatenate([qi_ref[0, :, p * LANES:(p + 1) * LANES] for p in (2 * u, 2 * u + 1)], axis=0)
           for u in range(pairs // 2)]

    def idx_body(c, carry):
        off = pl.multiple_of(c * tk, tk)
        units = [(u, par) for u in range(pairs // 2) for par in (0, 1)]

        def logits(u, par):
            kref = klo_ref if par == 0 else khi_ref
            return lax.dot_general(kref[0, pl.ds(off, tk), :], qi2[u], _NT, preferred_element_type=F32)

        sc = jnp.zeros((tk, tq), F32)
        nxt = logits(*units[0])
        for n, (u, par) in enumerate(units):
            lg = nxt
            if n + 1 < len(units):
                nxt = logits(*units[n + 1])
            for j in range(2):
                h = 2 * (2 * u + j) + par
                sc = sc + wrow[h] * jnp.maximum(lg[:, j * tq:(j + 1) * tq], 0.0)
        sc_ref[pl.ds(off, tk), :] = jnp.where(c * tk + kr <= qpos, sc, -jnp.inf)
        return carry

    lax.fori_loop(0, n_chunks, idx_body, 0)

    def float_of(u):
        ci = u ^ INT_MIN
        return lax.bitcast_convert_type(ci ^ ((ci >> 31) & INT_MAX), F32)

    srow = lax.broadcasted_iota(I32, (slab, tq), 0)

    def count(pred):
        def body(c, acc):
            for r in range(tk // slab):
                off = pl.multiple_of(c * tk + r * slab, slab)
                acc = acc + jnp.where(pred(sc_ref[pl.ds(off, slab), :], off + srow), 1, 0)
            return acc
        acc = lax.fori_loop(0, n_chunks, body, jnp.zeros((slab, tq), I32))
        return jnp.sum(acc, axis=0, keepdims=True)

    total = n_chunks * tk

    def count_ge(cf):
        return total - count(lambda s, kp: s < cf)

    def bit_body(b, carry):
        u, cnt = carry
        cand = u | lax.shift_left(jnp.int32(1), 31 - b)
        cc = count_ge(float_of(cand))
        ok = cc >= n_top
        return jnp.where(ok, cand, u), jnp.where(ok, cc, cnt)

    u, cnt = lax.fori_loop(0, 32, bit_body, (jnp.zeros((1, tq), I32), jnp.full((1, tq), total, I32)))
    thr = jnp.maximum(float_of(u), f32_lowest)

    tie = jnp.logical_and(cnt > n_top, float_of(u) >= f32_lowest)
    any_tie = jnp.max(jnp.where(tie, 1, 0)) > 0

    @pl.when(any_tie)
    def _():
        need = n_top - count(lambda s, kp: s > thr)
        keep_below = jnp.where(tie, need, INT_MAX).astype(F32)
        sub = 128
        before = (lax.broadcasted_iota(I32, (sub, sub), 1) < lax.broadcasted_iota(I32, (sub, sub), 0))
        lower = jnp.where(before, 1.0, 0.0).astype(BF16)

        def drop_body(c, seen):
            for r in range(tk // sub):
                off = pl.multiple_of(c * tk + r * sub, sub)
                sc = sc_ref[pl.ds(off, sub), :]
                tied = jnp.where(sc == thr, 1.0, 0.0)
                rank = jnp.dot(lower, tied.astype(BF16), preferred_element_type=F32) + seen
                drop = tied * jnp.where(rank >= keep_below, 1.0, 0.0)
                sc_ref[pl.ds(off, sub), :] = jnp.where(drop > 0.0, -jnp.inf, sc)
                seen = seen + jnp.sum(tied, axis=0, keepdims=True)
            return seen

        lax.fori_loop(0, n_chunks, drop_body, jnp.zeros((1, tq), F32))

    m_ref[...] = jnp.full(m_ref.shape, NEG_INF, F32)
    l_ref[...] = jnp.zeros(l_ref.shape, F32)
    acc_ref[...] = jnp.zeros(acc_ref.shape, F32)
    n_units = DSA_HEADS // 2
    q2 = [jnp.concatenate([qb_ref[0, :, h * HEAD_DIM:(h + 1) * HEAD_DIM] for h in (2 * u, 2 * u + 1)], axis=0)
          for u in range(n_units)]

    def kv_cols(u):
        g = (2 * u) // DSA_GROUP
        return slice(g * HEAD_DIM, (g + 1) * HEAD_DIM)

    all_units = tuple(range(n_units))

    def selection_bias(c):
        off = pl.multiple_of(c * tk, tk)
        bias = jnp.where(sc_ref[pl.ds(off, tk), :] >= thr, 0.0, -jnp.inf)
        return jnp.concatenate([bias, bias], axis=1)

    def stage_a(c, s_ref, bias2, units=all_units):
        off = pl.multiple_of(c * tk, tk)
        for u in units:
            s_ref[u] = bias2 + lax.dot_general(kb_ref[0, pl.ds(off, tk), kv_cols(u)], q2[u], _NT,
                                               preferred_element_type=F32)

    def stage_b(c, s_ref, units=all_units):
        off = pl.multiple_of(c * tk, tk)
        for u in units:
            mx = s_ref[u, 0:slab, :]
            for r in range(1, tk // slab):
                mx = jnp.maximum(mx, s_ref[u, r * slab:(r + 1) * slab, :])
            m_old = m_ref[u]
            m_new = jnp.maximum(m_old, jnp.max(mx, axis=0, keepdims=True))
            alpha = jnp.exp2(m_old - m_new)
            part = jnp.zeros((slab, 2 * tq), F32)
            for r in range(tk // slab):
                p = jnp.exp2(s_ref[u, r * slab:(r + 1) * slab, :] - m_new)
                part = part + p
                p_ref[u, r * slab:(r + 1) * slab, :] = p.astype(BF16)
            l_ref[u] = alpha * l_ref[u] + jnp.sum(part, axis=0, keepdims=True)
            acc_ref[u] = alpha * acc_ref[u] + lax.dot_general(
                vb_ref[0, pl.ds(off, tk), kv_cols(u)], p_ref[u], _TN, preferred_element_type=F32)
            m_ref[u] = m_new

    def stage_ab(ca, sa_ref, cb, sb_ref):
        bias2 = selection_bias(ca)
        for u in all_units:
            stage_a(ca, sa_ref, bias2, units=(u,))
            stage_b(cb, sb_ref, units=(u,))

    stage_a(0, s0_ref, selection_bias(0))

    def pair_body(t, carry):
        c = 2 * t
        stage_ab(c + 1, s1_ref, c, s0_ref)
        stage_ab(jnp.minimum(c + 2, n_chunks - 1), s0_ref, c + 1, s1_ref)
        return carry

    lax.fori_loop(0, n_chunks // 2, pair_body, 0)

    @pl.when(n_chunks % 2 == 1)
    def _():
        stage_b(n_chunks - 1, s0_ref)

    for u in range(n_units):
        o_t = acc_ref[u] / l_ref[u]
        for j in range(2):
            h = 2 * u + j
            o_ref[0, :, h * HEAD_DIM:(h + 1) * HEAD_DIM] = o_t[:, j * tq:(j + 1) * tq].T


def _dsa(qb, kb, vb, qi, klo, khi, wi):
    b, s, _ = qb.shape
    tq, tk = _DSA_TQ, _DSA_TK
    assert s % tk == 0
    n_top = min(DSA_TOPK_MAX, s // 4)
    assert tk >= n_top
    qrow = lambda width: pl.BlockSpec((1, tq, width), lambda bi, i: (bi, i, 0))
    full = lambda width: _resident((1, s, width), lambda bi, i: (bi, 0, 0))
    return pl.pallas_call(
        functools.partial(_dsa_kernel, n_top=n_top),
        grid=(b, s // tq),
        in_specs=[qrow(DSA_HEADS * HEAD_DIM), full(DSA_KV_HEADS * HEAD_DIM), full(DSA_KV_HEADS * HEAD_DIM),
                  qrow(IDX_HEADS * IDX_DIM), full(LANES), full(LANES), qrow(LANES)],
        out_specs=qrow(DSA_HEADS * HEAD_DIM),
        out_shape=jax.ShapeDtypeStruct((b, s, DSA_HEADS * HEAD_DIM), F32),
        scratch_shapes=[pltpu.VMEM((s, tq), F32),
                        pltpu.VMEM((DSA_HEADS // 2, 1, 2 * tq), F32),
                        pltpu.VMEM((DSA_HEADS // 2, 1, 2 * tq), F32),
                        pltpu.VMEM((DSA_HEADS // 2, HEAD_DIM, 2 * tq), F32),
                        pltpu.VMEM((DSA_HEADS // 2, tk, 2 * tq), F32), pltpu.VMEM((DSA_HEADS // 2, tk, 2 * tq), F32),
                        pltpu.VMEM((DSA_HEADS // 2, tk, 2 * tq), BF16)],
        compiler_params=_params(2),
        name="dsa_attn",
    )(qb, kb, vb, qi, klo, khi, wi)


def _rms(z, g):
    ms = jnp.mean(z * z, axis=-1, keepdims=True)
    return z * lax.rsqrt(ms + NORM_EPS) * g


def _outproj_kernel(ya_ref, yb_ref, ga_ref, gb_ref, x_ref, gt_ref, g_ref, w_ref, o_ref):
    mixed = ga_ref[0] * ya_ref[0] + gb_ref[0] * yb_ref[0]
    z = jnp.dot(mixed.astype(BF16), w_ref[...], preferred_element_type=F32)
    o_ref[0] = x_ref[0] + gt_ref[0] * _rms(z, g_ref[...])


def _outproj(ya, yb, ga, gb, x, gt, g, w, tm):
    b, s, d = x.shape
    row = pl.BlockSpec((1, tm, d), lambda bi, i: (bi, i, 0))
    return pl.pallas_call(
        _outproj_kernel,
        grid=(b, s // tm),
        in_specs=[row, row, row, row, row, pl.BlockSpec((1, 1, d), lambda bi, i: (bi, 0, 0)),
                  pl.BlockSpec((1, d), lambda bi, i: (0, 0)), _resident((d, d), lambda bi, i: (0, 0))],
        out_specs=row,
        out_shape=jax.ShapeDtypeStruct((b, s, d), F32),
        compiler_params=_params(2),
        name="out_proj",
    )(ya, yb, ga, gb, x, gt, g.reshape(1, d), w)


_FF_TILE = 256


def _ffn_kernel(x_ref, sc_ref, sh_ref, gt_ref, gpre_ref, gpost_ref, wg_ref, wu_ref, wd_ref, o_ref, a_ref):
    x = x_ref[0]
    h = (_rms(x, gpre_ref[...]) * (1.0 + sc_ref[0]) + sh_ref[0]).astype(BF16)
    for c0 in range(0, D_FF, _FF_TILE):
        gte = jnp.dot(h, wg_ref[:, c0:c0 + _FF_TILE], preferred_element_type=F32)
        up = jnp.dot(h, wu_ref[:, c0:c0 + _FF_TILE], preferred_element_type=F32)
        a_ref[:, c0:c0 + _FF_TILE] = (gte * jax.nn.sigmoid(gte) * up).astype(BF16)
    f = jnp.dot(a_ref[...], wd_ref[...], preferred_element_type=F32)
    o_ref[0] = x + gt_ref[0] * _rms(f, gpost_ref[...])


def _ffn(x, sc, sh, gt, gpre, gpost, wg, wu, wd, tm):
    b, s, d = x.shape
    ff = wg.shape[1]
    row = pl.BlockSpec((1, tm, d), lambda bi, i: (bi, i, 0))
    mod = pl.BlockSpec((1, 1, d), lambda bi, i: (bi, 0, 0))
    gain = pl.BlockSpec((1, d), lambda bi, i: (0, 0))
    return pl.pallas_call(
        _ffn_kernel,
        grid=(b, s // tm),
        in_specs=[row, mod, mod, mod, gain, gain,
                  _resident((d, ff), lambda bi, i: (0, 0)), _resident((d, ff), lambda bi, i: (0, 0)),
                  _resident((ff, d), lambda bi, i: (0, 0))],
        out_specs=row,
        out_shape=jax.ShapeDtypeStruct((b, s, d), F32),
        scratch_shapes=[pltpu.VMEM((tm, ff), BF16)],
        compiler_params=_params(2),
        name="ffn",
    )(x, sc, sh, gt, gpre.reshape(1, d), gpost.reshape(1, d), wg, wu, wd)


def kernel(x, c, positions, w_ada, b_ada, g_pre_mix, g_post_mix, w_in, w_out, g_pre_ffn, g_post_ffn,
           w_gate, w_up, w_down):
    b, s, d = x.shape
    depth = w_ada.shape[0]
    tm = min(s, 256)

    c_pad = jnp.zeros((8, d), F32).at[:b].set(c)
    mod = _ada(c_pad, w_ada, b_ada)[:, :b]
    tabs = _rope_tables(positions)

    for l in range(depth):
        sh1, sc1, gt1, sh2, sc2, gt2 = [mod[l, :, j * d:(j + 1) * d].reshape(b, 1, d) for j in range(N_MOD)]
        pr = _inproj(x, sc1, sh1, g_pre_mix[l], _arrange_w_in(w_in[l]), tabs, tm)
        ya = _moba(pr["qa"], pr["ka"], pr["va"])
        yb = _dsa(pr["qb"], pr["kb"], pr["vb"], pr["qi"], pr["klo"], pr["khi"], pr["wi"])
        x = _outproj(ya, yb, pr["ga"], pr["gb"], x, gt1, g_post_mix[l], w_out[l].astype(BF16), tm)
        x = _ffn(x, sc2, sh2, gt2, g_pre_ffn[l], g_post_ffn[l],
                 w_gate[l].astype(BF16), w_up[l].astype(BF16), w_down[l].astype(BF16), tm)
    return x
```
